```python
import jax, jax.numpy as jnp
from jax import lax
import numpy as np

D_MODEL = 1024
BATCH = 8
SEQ = 4096
DEPTH = 2
DEC_BATCH = 16
DEC_SEQ = 4096
PAST_LEN = 128

GRID_W = 64
ROPE_THETA = 10000.0
EPS = 1e-6
QBLOCK = 128

MLA_HEADS = 8
MLA_NOPE = 64
MLA_ROPE = 32
MLA_V = 64
Q_LORA = 256
KV_LORA = 128

GQA_HEADS = 8
GQA_KV_HEADS = 2
GQA_HD = 64

MIX_WIDTH = MLA_HEADS * MLA_V + GQA_HEADS * GQA_HD
IN_SPLITS = (Q_LORA, KV_LORA, MLA_ROPE, GQA_HEADS * GQA_HD, GQA_KV_HEADS * GQA_HD, GQA_KV_HEADS * GQA_HD)
IN_COLS = sum(IN_SPLITS)

D_FF = 2816

kernel_name = "hybrid_mla_gqa_macaron_encoder"


def rms_norm(x, g):
    xf = x.astype(jnp.float32)
    y = xf * lax.rsqrt(jnp.mean(xf * xf, axis=-1, keepdims=True) + EPS)
    return (y * g.astype(jnp.float32)).astype(x.dtype)


def swiglu(x, w_gate, w_up, w_down):
    return (jax.nn.silu(x @ w_gate) * (x @ w_up)) @ w_down


def axial_angles(seq, dim):
    rows = seq // GRID_W
    row = jnp.repeat(jnp.arange(rows, dtype=jnp.float32), GRID_W)
    col = jnp.tile(jnp.arange(GRID_W, dtype=jnp.float32), rows)
    half = dim // 2
    inv = ROPE_THETA ** (-jnp.arange(0, half, 2, dtype=jnp.float32) / half)
    return row[:, None] * inv[None, :], col[:, None] * inv[None, :]


def rope_rotate(x, ang):
    x1, x2 = jnp.split(x, 2, axis=-1)
    c = jnp.cos(ang)[None, :, None, :]
    s = jnp.sin(ang)[None, :, None, :]
    return jnp.concatenate([x1 * c - x2 * s, x2 * c + x1 * s], axis=-1)


def axial_rope(x):
    seq, dim = x.shape[1], x.shape[-1]
    ang_r, ang_c = axial_angles(seq, dim)
    xr, xc = jnp.split(x.astype(jnp.float32), 2, axis=-1)
    out = jnp.concatenate([rope_rotate(xr, ang_r), rope_rotate(xc, ang_c)], axis=-1)
    return out.astype(x.dtype)


def block_attention(q, k, v):
    b, s, hq, dq = q.shape
    hk, dv = k.shape[2], v.shape[-1]
    g = hq // hk
    nb = s // QBLOCK
    scale = dq ** -0.5
    qb = q.reshape(b, nb, QBLOCK, hk, g, dq).transpose(1, 0, 2, 3, 4, 5)

    def one_block(qi):
        sc = jnp.einsum('bqhgd,bkhd->bhgqk', qi, k).astype(jnp.float32) * scale
        p = jax.nn.softmax(sc, axis=-1).astype(v.dtype)
        return jnp.einsum('bhgqk,bkhd->bqhgd', p, v)

    out = lax.map(one_block, qb)
    return out.transpose(1, 0, 2, 3, 4, 5).reshape(b, s, hq, dv)


def token_mix(h, w_in, q_a_norm, w_q_b, kv_a_norm, w_kv_b, gqa_q_norm, gqa_k_norm, w_out):
    b, s, _ = h.shape
    z = h @ w_in
    idx = list(np.cumsum(IN_SPLITS)[:-1])
    c_q, c_kv, k_pe, q_g, k_g, v_g = jnp.split(z, idx, axis=-1)

    qa = (rms_norm(c_q, q_a_norm) @ w_q_b).reshape(b, s, MLA_HEADS, MLA_NOPE + MLA_ROPE)
    q_nope, q_pe = jnp.split(qa, [MLA_NOPE], axis=-1)
    q_mla = jnp.concatenate([q_nope, axial_rope(q_pe)], axis=-1)
    kv = (rms_norm(c_kv, kv_a_norm) @ w_kv_b).reshape(b, s, MLA_HEADS, MLA_NOPE + MLA_V)
    k_nope, v_mla = jnp.split(kv, [MLA_NOPE], axis=-1)
    k_pe = axial_rope(k_pe.reshape(b, s, 1, MLA_ROPE))
    k_mla = jnp.concatenate([k_nope, jnp.broadcast_to(k_pe, (b, s, MLA_HEADS, MLA_ROPE))], axis=-1)
    o_mla = block_attention(q_mla, k_mla, v_mla).reshape(b, s, MLA_HEADS * MLA_V)

    q_b = axial_rope(rms_norm(q_g.reshape(b, s, GQA_HEADS, GQA_HD), gqa_q_norm))
    k_b = axial_rope(rms_norm(k_g.reshape(b, s, GQA_KV_HEADS, GQA_HD), gqa_k_norm))
    v_b = v_g.reshape(b, s, GQA_KV_HEADS, GQA_HD)
    o_gqa = block_attention(q_b, k_b, v_b).reshape(b, s, GQA_HEADS * GQA_HD)

    return jnp.concatenate([o_mla, o_gqa], axis=-1) @ w_out


def trunk(x, norm_ffn1, w_ffn1_gate, w_ffn1_up, w_ffn1_down,
          norm_mix, w_in, q_a_norm, w_q_b, kv_a_norm, w_kv_b,
          gqa_q_norm, gqa_k_norm, w_out,
          norm_ffn2, w_ffn2_gate, w_ffn2_up, w_ffn2_down, final_norm):
    for l in range(DEPTH):
        x = x + 0.5 * swiglu(rms_norm(x, norm_ffn1[l]), w_ffn1_gate[l], w_ffn1_up[l], w_ffn1_down[l])
        x = x + token_mix(rms_norm(x, norm_mix[l]), w_in[l], q_a_norm[l], w_q_b[l], kv_a_norm[l],
                          w_kv_b[l], gqa_q_norm[l], gqa_k_norm[l], w_out[l])
        x = x + 0.5 * swiglu(rms_norm(x, norm_ffn2[l]), w_ffn2_gate[l], w_ffn2_up[l], w_ffn2_down[l])
    return rms_norm(x, final_norm)


def setup_inputs(seed: int = 0) -> dict:
    key = jax.random.key(seed)
    ks = jax.random.split(key, 24)
    f32 = jnp.float32

    def w(k, shape):
        return jax.random.normal(k, shape, f32) * (shape[-2] ** -0.5)

    def gain(k, shape):
        return 1.0 + 0.02 * jax.random.normal(k, shape, f32)

    L, D = DEPTH, D_MODEL
    return {
        "x_prompt": jax.random.normal(ks[0], (BATCH, SEQ, D), f32),
        "x_sample": jax.random.normal(ks[1], (DEC_BATCH, DEC_SEQ, D), f32),
        "norm_ffn1": gain(ks[2], (L, D)),
        "w_ffn1_gate": w(ks[3], (L, D, D_FF)),
        "w_ffn1_up": w(ks[4], (L, D, D_FF)),
        "w_ffn1_down": w(ks[5], (L, D_FF, D)),
        "norm_mix": gain(ks[6], (L, D)),
        "w_in": w(ks[7], (L, D, IN_COLS)),
        "q_a_norm": gain(ks[8], (L, Q_LORA)),
        "w_q_b": w(ks[9], (L, Q_LORA, MLA_HEADS * (MLA_NOPE + MLA_ROPE))),
        "kv_a_norm": gain(ks[10], (L, KV_LORA)),
        "w_kv_b": w(ks[11], (L, KV_LORA, MLA_HEADS * (MLA_NOPE + MLA_V))),
        "gqa_q_norm": gain(ks[12], (L, GQA_HD)),
        "gqa_k_norm": gain(ks[13], (L, GQA_HD)),
        "w_out": w(ks[14], (L, MIX_WIDTH, D)),
        "norm_ffn2": gain(ks[15], (L, D)),
        "w_ffn2_gate": w(ks[16], (L, D, D_FF)),
        "w_ffn2_up": w(ks[17], (L, D, D_FF)),
        "w_ffn2_down": w(ks[18], (L, D_FF, D)),
        "final_norm": gain(ks[19], (D,)),
    }


def reference(x_prompt, x_sample, norm_ffn1, w_ffn1_gate, w_ffn1_up, w_ffn1_down,
              norm_mix, w_in, q_a_norm, w_q_b, kv_a_norm, w_kv_b,
              gqa_q_norm, gqa_k_norm, w_out,
              norm_ffn2, w_ffn2_gate, w_ffn2_up, w_ffn2_down, final_norm):
    y_prompt = trunk(x_prompt, norm_ffn1, w_ffn1_gate, w_ffn1_up, w_ffn1_down,
                     norm_mix, w_in, q_a_norm, w_q_b, kv_a_norm, w_kv_b,
                     gqa_q_norm, gqa_k_norm, w_out,
                     norm_ffn2, w_ffn2_gate, w_ffn2_up, w_ffn2_down, final_norm)
    y_sample = trunk(x_sample, norm_ffn1, w_ffn1_gate, w_ffn1_up, w_ffn1_down,
                     norm_mix, w_in, q_a_norm, w_q_b, kv_a_norm, w_kv_b,
                     gqa_q_norm, gqa_k_norm, w_out,
                     norm_ffn2, w_ffn2_gate, w_ffn2_up, w_ffn2_down, final_norm)
    return (y_prompt, y_sample)
```

```python
import functools

import jax
import jax.numpy as jnp
import numpy as np
from jax import lax
from jax.experimental import pallas as pl
from jax.experimental.pallas import tpu as pltpu

D_MODEL = 1024
GRID_W = 64
ROPE_THETA = 10000.0
EPS = 1e-6

MLA_HEADS = 8
MLA_NOPE = 64
MLA_ROPE = 32
MLA_V = 64
Q_LORA = 256
KV_LORA = 128
GQA_HEADS = 8
GQA_KV_HEADS = 2
GQA_HD = 64
D_FF = 2816

LANES = 128
LOG2E = 1.4426950408889634

ROW_TILE = 512
FF_CHUNK = 256
Q_TILE = 256
VMEM_LIMIT_BYTES = 56 * 1024 * 1024

_ZW = Q_LORA + KV_LORA + LANES + GQA_HEADS * GQA_HD + 2 * LANES
_Z_CQ = (0, Q_LORA)
_Z_CKV = (Q_LORA, Q_LORA + KV_LORA)
_Z_KPE = (_Z_CKV[1], _Z_CKV[1] + LANES)
_Z_QG = (_Z_KPE[1], _Z_KPE[1] + GQA_HEADS * GQA_HD)
_Z_KG = (_Z_QG[1], _Z_QG[1] + LANES)
_Z_VG = (_Z_KG[1], _Z_KG[1] + LANES)

BF16 = jnp.bfloat16
F32 = jnp.float32


def _rms(x, g):
    ms = jnp.mean(x * x, axis=-1, keepdims=True)
    return x * lax.rsqrt(ms + EPS) * g


def _rope(x, c, sa, sb, d):
    return x * c + pltpu.roll(x, LANES - d, 1) * sa + pltpu.roll(x, d, 1) * sb


def _ffn_kernel(*refs, has_attn, final):
    it = iter(refs)
    x_ref = next(it)
    if has_attn:
        om_ref, og_ref, wo_ref = next(it), next(it), next(it)
    g_ref, wgu_ref, wd_ref = next(it), next(it), next(it)
    if final:
        fg_ref = next(it)
    o_ref = next(it)
    a_ref = next(it)

    x = x_ref[...]
    if has_attn:
        half = om_ref.shape[1]
        x = x + jnp.dot(om_ref[...], wo_ref[:half, :], preferred_element_type=F32)
        x = x + jnp.dot(og_ref[...], wo_ref[half:, :], preferred_element_type=F32)
    hb = _rms(x, g_ref[...]).astype(BF16)
    for c in range(D_FF // FF_CHUNK):
        gu = jnp.dot(hb, wgu_ref[:, c * 2 * FF_CHUNK:(c + 1) * 2 * FF_CHUNK], preferred_element_type=F32)
        g = gu[:, :FF_CHUNK]
        u = gu[:, FF_CHUNK:]
        a_ref[:, c * FF_CHUNK:(c + 1) * FF_CHUNK] = (g * jax.nn.sigmoid(g) * u).astype(BF16)
    y = x + 0.5 * jnp.dot(a_ref[...], wd_ref[...], preferred_element_type=F32)
    if final:
        y = _rms(y, fg_ref[...])
    o_ref[...] = y


def _const_spec(shape):
    return pl.BlockSpec(shape, lambda *_: (0,) * len(shape), pipeline_mode=pl.Buffered(1))


def _ffn_call(x, g, wgu, wd, attn=None, final_g=None):
    t = x.shape[0]
    row = lambda w: pl.BlockSpec((ROW_TILE, w), lambda i: (i, 0))
    args, specs = [x], [row(D_MODEL)]
    if attn is not None:
        o_mla, o_gqa, w_out = attn
        args += [o_mla, o_gqa, w_out]
        specs += [row(o_mla.shape[1]), row(o_gqa.shape[1]), _const_spec(w_out.shape)]
    args += [g, wgu, wd]
    specs += [_const_spec(g.shape), _const_spec(wgu.shape), _const_spec(wd.shape)]
    if final_g is not None:
        args.append(final_g)
        specs.append(_const_spec(final_g.shape))
    return pl.pallas_call(
        functools.partial(_ffn_kernel, has_attn=attn is not None, final=final_g is not None),
        grid=(t // ROW_TILE,),
        in_specs=specs,
        out_specs=row(D_MODEL),
        out_shape=jax.ShapeDtypeStruct((t, D_MODEL), F32),
        scratch_shapes=[pltpu.VMEM((ROW_TILE, D_FF), BF16)],
        compiler_params=pltpu.CompilerParams(vmem_limit_bytes=VMEM_LIMIT_BYTES),
        name="ffn_attn" if attn is not None else "ffn",
    )(*args)


def _prep_kernel(x_ref, tab_ref, gmix_ref, win_ref, gqa_ref, wqb_ref, gkv_ref, wkvb_ref, gq_ref, gk_ref,
                 gmat_ref, qm_ref, km_ref, vm_ref, qg_ref, kg_ref, vg_ref):
    hb = _rms(x_ref[...], gmix_ref[...]).astype(BF16)
    z = jnp.dot(hb, win_ref[...], preferred_element_type=F32)

    mc, msa, msb = (tab_ref[:, j * LANES:(j + 1) * LANES] for j in range(3))
    gc, gsa, gsb = (tab_ref[:, j * LANES:(j + 1) * LANES] for j in range(3, 6))
    lane = lax.broadcasted_iota(jnp.int32, (1, LANES), 1)
    lo = lane < GQA_HD
    one_a = (lane == MLA_V).astype(F32)
    one_b = (lane == 0).astype(F32)

    cq = _rms(z[:, _Z_CQ[0]:_Z_CQ[1]], gqa_ref[...]).astype(BF16)
    qa = jnp.dot(cq, wqb_ref[...], preferred_element_type=F32)
    q_scale = (MLA_NOPE + MLA_ROPE) ** -0.5 * LOG2E
    for h in range(MLA_HEADS):
        sl = slice(h * LANES, (h + 1) * LANES)
        qm_ref[:, sl] = (_rope(qa[:, sl], mc, msa, msb, MLA_ROPE // 4) * q_scale).astype(BF16)

    ckv = _rms(z[:, _Z_CKV[0]:_Z_CKV[1]], gkv_ref[...]).astype(BF16)
    kv = jnp.dot(ckv, wkvb_ref[...], preferred_element_type=F32)
    kpe = _rope(z[:, _Z_KPE[0]:_Z_KPE[1]], mc, msa, msb, MLA_ROPE // 4)
    for h in range(MLA_HEADS):
        sl = slice(h * LANES, (h + 1) * LANES)
        km_ref[:, sl] = (kv[:, sl] + kpe).astype(BF16)
        vsl = slice((MLA_HEADS + h) * LANES, (MLA_HEADS + h + 1) * LANES)
        vm_ref[:, sl] = (kv[:, vsl] + (one_a if h % 2 == 0 else one_b)).astype(BF16)

    def head_mean_sq(v, gmat):
        sq = v * v
        hi = sq.astype(BF16)
        lo_part = (sq - hi.astype(F32)).astype(BF16)
        return (jnp.dot(hi, gmat, preferred_element_type=F32)
                + jnp.dot(lo_part, gmat, preferred_element_type=F32))

    qg = z[:, _Z_QG[0]:_Z_QG[1]]
    qg = qg * lax.rsqrt(head_mean_sq(qg, gmat_ref[...]) + EPS) * gq_ref[...]
    for j in range(GQA_HEADS // 2):
        sl = slice(j * LANES, (j + 1) * LANES)
        qg_ref[:, sl] = _rope(qg[:, sl], gc, gsa, gsb, GQA_HD // 4).astype(BF16)

    kg = z[:, _Z_KG[0]:_Z_KG[1]]
    kg = kg * lax.rsqrt(head_mean_sq(kg, gmat_ref[:LANES, :LANES]) + EPS) * gk_ref[...]
    kg = _rope(kg, gc, gsa, gsb, GQA_HD // 4)
    kg_sw = pltpu.roll(kg, GQA_HD, 1)
    zero = jnp.zeros_like(kg)
    kg_ref[:, 0 * LANES:1 * LANES] = jnp.where(lo, kg, zero).astype(BF16)
    kg_ref[:, 1 * LANES:2 * LANES] = jnp.where(lo, zero, kg_sw).astype(BF16)
    kg_ref[:, 2 * LANES:3 * LANES] = jnp.where(lo, kg_sw, zero).astype(BF16)
    kg_ref[:, 3 * LANES:4 * LANES] = jnp.where(lo, zero, kg).astype(BF16)

    vg = z[:, _Z_VG[0]:_Z_VG[1]]
    vg_sw = pltpu.roll(vg, GQA_HD, 1)
    vg_ref[:, 0 * LANES:1 * LANES] = jnp.where(lo, vg, one_a).astype(BF16)
    vg_ref[:, 1 * LANES:2 * LANES] = jnp.where(lo, one_b, vg_sw).astype(BF16)
    vg_ref[:, 2 * LANES:3 * LANES] = jnp.where(lo, vg_sw, one_a).astype(BF16)
    vg_ref[:, 3 * LANES:4 * LANES] = jnp.where(lo, one_b, vg).astype(BF16)


def _prep_call(x, tab, seq, p):
    t = x.shape[0]
    tiles_per_seq = seq // ROW_TILE
    row = lambda w: pl.BlockSpec((ROW_TILE, w), lambda i: (i, 0))
    consts = [p["gmix"], p["win"], p["gqa"], p["wqb"], p["gkv"], p["wkvb"], p["gq"], p["gk"], p["gmat"]]
    widths = (MLA_HEADS * LANES,) * 3 + (GQA_HEADS // 2 * LANES,) * 3
    return pl.pallas_call(
        _prep_kernel,
        grid=(t // ROW_TILE,),
        in_specs=[row(D_MODEL), pl.BlockSpec((ROW_TILE, tab.shape[1]), lambda i: (i % tiles_per_seq, 0))]
        + [_const_spec(c.shape) for c in consts],
        out_specs=[row(w) for w in widths],
        out_shape=[jax.ShapeDtypeStruct((t, w), BF16) for w in widths],
        compiler_params=pltpu.CompilerParams(vmem_limit_bytes=VMEM_LIMIT_BYTES),
        name="prep",
    )(x, tab, *consts)


def _attn_kernel(q_ref, k_ref, v_ref, o_ref):
    lane = lax.broadcasted_iota(jnp.int32, (1, LANES), 1)
    outs = []
    for j in range(2):
        q = q_ref[:, :LANES] if j == 0 else q_ref[:, q_ref.shape[1] - LANES:]
        k = k_ref[:, j * LANES:(j + 1) * LANES]
        v = v_ref[:, j * LANES:(j + 1) * LANES]
        s = lax.dot_general(q, k, (((1,), (1,)), ((), ())), preferred_element_type=F32)
        m = jnp.max(s, axis=-1, keepdims=True)
        p = jnp.exp2(s - m).astype(BF16)
        acc = jnp.dot(p, v, preferred_element_type=F32)
        one_lane = MLA_V if j == 0 else 0
        outs.append(acc / acc[:, one_lane:one_lane + 1])
    o_ref[...] = jnp.where(lane < MLA_V, outs[0], outs[1]).astype(BF16)


def _attn_call(q, k, v, batch, seq, kv_share, name):
    pairs = 4
    qw = q.shape[1] // pairs
    nq = seq // Q_TILE
    kvw = 2 * LANES
    return pl.pallas_call(
        _attn_kernel,
        grid=(batch, pairs, nq),
        in_specs=[
            pl.BlockSpec((Q_TILE, qw), lambda b, p, i: (b * nq + i, p)),
            pl.BlockSpec((seq, kvw), lambda b, p, i: (b, p // kv_share)),
            pl.BlockSpec((seq, kvw), lambda b, p, i: (b, p // kv_share)),
        ],
        out_specs=pl.BlockSpec((Q_TILE, LANES), lambda b, p, i: (b * nq + i, p)),
        out_shape=jax.ShapeDtypeStruct((q.shape[0], pairs * LANES), BF16),
        compiler_params=pltpu.CompilerParams(vmem_limit_bytes=VMEM_LIMIT_BYTES),
        name=name,
    )(q, k, v)


def _rope_tables(seq):
    rows = seq // GRID_W
    row = jnp.repeat(jnp.arange(rows, dtype=F32), GRID_W)
    col = jnp.tile(jnp.arange(GRID_W, dtype=F32), rows)

    def slot_tables(dim, lead, slot_w):
        half = dim // 2
        inv = ROPE_THETA ** (-jnp.arange(0, half, 2, dtype=F32) / half)
        ar, ac = row[:, None] * inv[None, :], col[:, None] * inv[None, :]
        z = jnp.zeros_like(ar)
        c = jnp.concatenate([jnp.cos(ar)] * 2 + [jnp.cos(ac)] * 2, axis=1)
        sa = jnp.concatenate([-jnp.sin(ar), z, -jnp.sin(ac), z], axis=1)
        sb = jnp.concatenate([z, jnp.sin(ar), z, jnp.sin(ac)], axis=1)
        tail = slot_w - lead - dim
        pad = lambda a, fill: jnp.concatenate(
            [jnp.full((seq, lead), fill, F32), a, jnp.full((seq, tail), fill, F32)], axis=1)
        return pad(c, 1.0), pad(sa, 0.0), pad(sb, 0.0)

    mla = slot_tables(MLA_ROPE, MLA_NOPE, LANES)
    gqa = [jnp.tile(a, (1, LANES // GQA_HD)) for a in slot_tables(GQA_HD, 0, GQA_HD)]
    return jnp.concatenate(list(mla) + gqa, axis=1)


def _layer_params(l, norm_ffn1, w_ffn1_gate, w_ffn1_up, w_ffn1_down, norm_mix, w_in, q_a_norm, w_q_b,
                  kv_a_norm, w_kv_b, gqa_q_norm, gqa_k_norm, w_out, norm_ffn2, w_ffn2_gate, w_ffn2_up,
                  w_ffn2_down):
    nch = D_FF // FF_CHUNK

    def gate_up(wg, wu):
        w = jnp.stack([wg.reshape(D_MODEL, nch, FF_CHUNK), wu.reshape(D_MODEL, nch, FF_CHUNK)], axis=2)
        return w.reshape(D_MODEL, 2 * D_FF).astype(BF16)

    row = lambda v: v.reshape(1, -1).astype(F32)
    wi = w_in[l]
    o = np.cumsum([0, Q_LORA, KV_LORA, MLA_ROPE, GQA_HEADS * GQA_HD, GQA_KV_HEADS * GQA_HD, GQA_KV_HEADS * GQA_HD])
    zc = lambda n: jnp.zeros((D_MODEL, n), F32)
    win = jnp.concatenate(
        [wi[:, o[0]:o[1]], wi[:, o[1]:o[2]], zc(MLA_NOPE), wi[:, o[2]:o[3]], zc(LANES - MLA_NOPE - MLA_ROPE),
         wi[:, o[3]:o[4]], wi[:, o[4]:o[5]], wi[:, o[5]:o[6]]], axis=1).astype(BF16)
    assert win.shape[1] == _ZW

    wqb = w_q_b[l].reshape(Q_LORA, MLA_HEADS, MLA_NOPE + MLA_ROPE)
    wqb = jnp.pad(wqb, ((0, 0), (0, 0), (0, LANES - MLA_NOPE - MLA_ROPE))).reshape(Q_LORA, MLA_HEADS * LANES)

    wkv = w_kv_b[l].reshape(KV_LORA, MLA_HEADS, MLA_NOPE + MLA_V)
    wk = jnp.pad(wkv[:, :, :MLA_NOPE], ((0, 0), (0, 0), (0, LANES - MLA_NOPE)))
    wv_a = jnp.pad(wkv[:, :, MLA_NOPE:], ((0, 0), (0, 0), (0, LANES - MLA_V)))
    wv_b = jnp.pad(wkv[:, :, MLA_NOPE:], ((0, 0), (0, 0), (LANES - MLA_V, 0)))
    odd = (jnp.arange(MLA_HEADS) % 2 == 1)[None, :, None]
    wkvb = jnp.concatenate([wk, jnp.where(odd, wv_b, wv_a)], axis=1).reshape(KV_LORA, 2 * MLA_HEADS * LANES)

    gmat = np.kron(np.eye(GQA_HEADS), np.full((GQA_HD, GQA_HD), 1.0 / GQA_HD))
    return dict(
        g1=row(norm_ffn1[l]), wgu1=gate_up(w_ffn1_gate[l], w_ffn1_up[l]), wd1=w_ffn1_down[l].astype(BF16),
        gmix=row(norm_mix[l]), win=win, gqa=row(q_a_norm[l]), wqb=wqb.astype(BF16), gkv=row(kv_a_norm[l]),
        wkvb=wkvb.astype(BF16),
        gq=row(jnp.tile(gqa_q_norm[l], GQA_HEADS)) * (GQA_HD ** -0.5 * LOG2E),
        gk=row(jnp.tile(gqa_k_norm[l], GQA_KV_HEADS)),
        gmat=jnp.asarray(gmat, BF16), wout=w_out[l].astype(BF16),
        g2=row(norm_ffn2[l]), wgu2=gate_up(w_ffn2_gate[l], w_ffn2_up[l]), wd2=w_ffn2_down[l].astype(BF16),
    )


def _trunk(x3, layers, tab, final_g):
    batch, seq, _ = x3.shape
    assert seq % ROW_TILE == 0 and seq % Q_TILE == 0 and seq % GRID_W == 0
    x = x3.reshape(batch * seq, D_MODEL)
    for li, p in enumerate(layers):
        x = _ffn_call(x, p["g1"], p["wgu1"], p["wd1"])
        qm, km, vm, qg, kg, vg = _prep_call(x, tab, seq, p)
        o_mla = _attn_call(qm, km, vm, batch, seq, 1, "attn_mla")
        o_gqa = _attn_call(qg, kg, vg, batch, seq, 2, "attn_gqa")
        x = _ffn_call(x, p["g2"], p["wgu2"], p["wd2"], attn=(o_mla, o_gqa, p["wout"]),
                      final_g=final_g if li == len(layers) - 1 else None)
    return x.reshape(batch, seq, D_MODEL)


def kernel(x_prompt, x_sample, norm_ffn1, w_ffn1_gate, w_ffn1_up, w_ffn1_down, norm_mix, w_in, q_a_norm, w_q_b, kv_a_norm, w_kv_b, gqa_q_norm, gqa_k_norm, w_out, norm_ffn2, w_ffn2_gate, w_ffn2_up, w_ffn2_down, final_norm):
    depth = norm_ffn1.shape[0]
    layers = [
        _layer_params(l, norm_ffn1, w_ffn1_gate, w_ffn1_up, w_ffn1_down, norm_mix, w_in, q_a_norm, w_q_b,
                      kv_a_norm, w_kv_b, gqa_q_norm, gqa_k_norm, w_out, norm_ffn2, w_ffn2_gate, w_ffn2_up,
                      w_ffn2_down)
        for l in range(depth)
    ]
    final_g = final_norm.reshape(1, -1).astype(F32)
    outs = []
    for x3 in (x_prompt, x_sample):
        tab = _rope_tables(x3.shape[1])
        outs.append(_trunk(x3, layers, tab, final_g))
    return tuple(outs)
```

```python
import functools

import jax
import jax.numpy as jnp
import numpy as np
from jax import lax
from jax.experimental import pallas as pl
from jax.experimental.pallas import tpu as pltpu

D_MODEL = 1024
GRID_W = 64
ROPE_THETA = 10000.0
EPS = 1e-6

MLA_HEADS = 8
MLA_NOPE = 64
MLA_ROPE = 32
MLA_V = 64
Q_LORA = 256
KV_LORA = 128
GQA_HEADS = 8
GQA_KV_HEADS = 2
GQA_HD = 64
D_FF = 2816

LANES = 128
LOG2E = 1.4426950408889634

ROW_TILE = 512
FF_CHUNK = 256
Q_TILE = 256
VMEM_LIMIT_BYTES = 56 * 1024 * 1024

_ZW = Q_LORA + KV_LORA + LANES + GQA_HEADS * GQA_HD + 2 * LANES
_Z_CQ = (0, Q_LORA)
_Z_CKV = (Q_LORA, Q_LORA + KV_LORA)
_Z_KPE = (_Z_CKV[1], _Z_CKV[1] + LANES)
_Z_QG = (_Z_KPE[1], _Z_KPE[1] + GQA_HEADS * GQA_HD)
_Z_KG = (_Z_QG[1], _Z_QG[1] + LANES)
_Z_VG = (_Z_KG[1], _Z_KG[1] + LANES)

BF16 = jnp.bfloat16
F32 = jnp.float32


def _rms(x, g):
    ms = jnp.mean(x * x, axis=-1, keepdims=True)
    return x * lax.rsqrt(ms + EPS) * g


def _rope(x, c, sa, sb, d):
    return x * c + pltpu.roll(x, LANES - d, 1) * sa + pltpu.roll(x, d, 1) * sb


def _ffn_kernel(*refs, has_attn, final):
    it = iter(refs)
    x_ref = next(it)
    if has_attn:
        om_ref, og_ref, wo_ref = next(it), next(it), next(it)
    g_ref, wgu_ref, wd_ref = next(it), next(it), next(it)
    if final:
        fg_ref = next(it)
    o_ref = next(it)
    a_ref = next(it)

    x = x_ref[...]
    if has_attn:
        half = om_ref.shape[1]
        x = x + jnp.dot(om_ref[...], wo_ref[:half, :], preferred_element_type=F32)
        x = x + jnp.dot(og_ref[...], wo_ref[half:, :], preferred_element_type=F32)
    hb = _rms(x, g_ref[...]).astype(BF16)
    for c in range(D_FF // FF_CHUNK):
        gu = jnp.dot(hb, wgu_ref[:, c * 2 * FF_CHUNK:(c + 1) * 2 * FF_CHUNK], preferred_element_type=F32)
        g = gu[:, :FF_CHUNK]
        u = gu[:, FF_CHUNK:]
        a_ref[:, c * FF_CHUNK:(c + 1) * FF_CHUNK] = (g * jax.nn.sigmoid(g) * u).astype(BF16)
    y = x + 0.5 * jnp.dot(a_ref[...], wd_ref[...], preferred_element_type=F32)
    if final:
        y = _rms(y, fg_ref[...])
    o_ref[...] = y


def _const_spec(shape):
    return pl.BlockSpec(shape, lambda *_: (0,) * len(shape), pipeline_mode=pl.Buffered(1))


def _ffn_call(x, g, wgu, wd, attn=None, final_g=None):
    t = x.shape[0]
    row = lambda w: pl.BlockSpec((ROW_TILE, w), lambda i: (i, 0))
    args, specs = [x], [row(D_MODEL)]
    if attn is not None:
        o_mla, o_gqa, w_out = attn
        args += [o_mla, o_gqa, w_out]
        specs += [row(o_mla.shape[1]), row(o_gqa.shape[1]), _const_spec(w_out.shape)]
    args += [g, wgu, wd]
    specs += [_const_spec(g.shape), _const_spec(wgu.shape), _const_spec(wd.shape)]
    if final_g is not None:
        args.append(final_g)
        specs.append(_const_spec(final_g.shape))
    return pl.pallas_call(
        functools.partial(_ffn_kernel, has_attn=attn is not None, final=final_g is not None),
        grid=(t // ROW_TILE,),
        in_specs=specs,
        out_specs=row(D_MODEL),
        out_shape=jax.ShapeDtypeStruct((t, D_MODEL), F32),
        scratch_shapes=[pltpu.VMEM((ROW_TILE, D_FF), BF16)],
        compiler_params=pltpu.CompilerParams(vmem_limit_bytes=VMEM_LIMIT_BYTES),
        name="ffn_attn" if attn is not None else "ffn",
    )(*args)


def _prep_kernel(x_ref, tab_ref, gmix_ref, win_ref, gqa_ref, wqb_ref, gkv_ref, wkvb_ref, gq_ref, gk_ref,
                 gmat_ref, qm_ref, km_ref, vm_ref, qg_ref, kg_ref, vg_ref):
    hb = _rms(x_ref[...], gmix_ref[...]).astype(BF16)
    z = jnp.dot(hb, win_ref[...], preferred_element_type=F32)

    mc, msa, msb = (tab_ref[:, j * LANES:(j + 1) * LANES] for j in range(3))
    gc, gsa, gsb = (tab_ref[:, j * LANES:(j + 1) * LANES] for j in range(3, 6))
    lane = lax.broadcasted_iota(jnp.int32, (1, LANES), 1)
    lo = lane < GQA_HD
    one_a = (lane == MLA_V).astype(F32)
    one_b = (lane == 0).astype(F32)

    cq = _rms(z[:, _Z_CQ[0]:_Z_CQ[1]], gqa_ref[...]).astype(BF16)
    qa = jnp.dot(cq, wqb_ref[...], preferred_element_type=F32)
    q_scale = (MLA_NOPE + MLA_ROPE) ** -0.5 * LOG2E
    for h in range(MLA_HEADS):
        sl = slice(h * LANES, (h + 1) * LANES)
        qm_ref[:, sl] = (_rope(qa[:, sl], mc, msa, msb, MLA_ROPE // 4) * q_scale).astype(BF16)

    ckv = _rms(z[:, _Z_CKV[0]:_Z_CKV[1]], gkv_ref[...]).astype(BF16)
    kv = jnp.dot(ckv, wkvb_ref[...], preferred_element_type=F32)
    kpe = _rope(z[:, _Z_KPE[0]:_Z_KPE[1]], mc, msa, msb, MLA_ROPE // 4)
    for h in range(MLA_HEADS):
        sl = slice(h * LANES, (h + 1) * LANES)
        km_ref[:, sl] = (kv[:, sl] + kpe).astype(BF16)
        vsl = slice((MLA_HEADS + h) * LANES, (MLA_HEADS + h + 1) * LANES)
        vm_ref[:, sl] = (kv[:, vsl] + (one_a if h % 2 == 0 else one_b)).astype(BF16)

    def head_mean_sq(v, gmat):
        sq = v * v
        hi = sq.astype(BF16)
        lo_part = (sq - hi.astype(F32)).astype(BF16)
        return (jnp.dot(hi, gmat, preferred_element_type=F32)
                + jnp.dot(lo_part, gmat, preferred_element_type=F32))

    qg = z[:, _Z_QG[0]:_Z_QG[1]]
    qg = qg * lax.rsqrt(head_mean_sq(qg, gmat_ref[...]) + EPS) * gq_ref[...]
    for j in range(GQA_HEADS // 2):
        sl = slice(j * LANES, (j + 1) * LANES)
        qg_ref[:, sl] = _rope(qg[:, sl], gc, gsa, gsb, GQA_HD // 4).astype(BF16)

    kg = z[:, _Z_KG[0]:_Z_KG[1]]
    kg = kg * lax.rsqrt(head_mean_sq(kg, gmat_ref[:LANES, :LANES]) + EPS) * gk_ref[...]
    kg = _rope(kg, gc, gsa, gsb, GQA_HD // 4)
    kg_sw = pltpu.roll(kg, GQA_HD, 1)
    zero = jnp.zeros_like(kg)
    kg_ref[:, 0 * LANES:1 * LANES] = jnp.where(lo, kg, zero).astype(BF16)
    kg_ref[:, 1 * LANES:2 * LANES] = jnp.where(lo, zero, kg_sw).astype(BF16)
    kg_ref[:, 2 * LANES:3 * LANES] = jnp.where(lo, kg_sw, zero).astype(BF16)
    kg_ref[:, 3 * LANES:4 * LANES] = jnp.where(lo, zero, kg).astype(BF16)

    vg = z[:, _Z_VG[0]:_Z_VG[1]]
    vg_sw = pltpu.roll(vg, GQA_HD, 1)
    vg_ref[:, 0 * LANES:1 * LANES] = jnp.where(lo, vg, one_a).astype(BF16)
    vg_ref[:, 1 * LANES:2 * LANES] = jnp.where(lo, one_b, vg_sw).astype(BF16)
    vg_ref[:, 2 * LANES:3 * LANES] = jnp.where(lo, vg_sw, one_a).astype(BF16)
    vg_ref[:, 3 * LANES:4 * LANES] = jnp.where(lo, one_b, vg).astype(BF16)


def _prep_call(x, tab, seq, p):
    t = x.shape[0]
    tiles_per_seq = seq // ROW_TILE
    row = lambda w: pl.BlockSpec((ROW_TILE, w), lambda i: (i, 0))
    consts = [p["gmix"], p["win"], p["gqa"], p["wqb"], p["gkv"], p["wkvb"], p["gq"], p["gk"], p["gmat"]]
    widths = (MLA_HEADS * LANES,) * 3 + (GQA_HEADS // 2 * LANES,) * 3
    return pl.pallas_call(
        _prep_kernel,
        grid=(t // ROW_TILE,),
        in_specs=[row(D_MODEL), pl.BlockSpec((ROW_TILE, tab.shape[1]), lambda i: (i % tiles_per_seq, 0))]
        + [_const_spec(c.shape) for c in consts],
        out_specs=[row(w) for w in widths],
        out_shape=[jax.ShapeDtypeStruct((t, w), BF16) for w in widths],
        compiler_params=pltpu.CompilerParams(vmem_limit_bytes=VMEM_LIMIT_BYTES),
        name="prep",
    )(x, tab, *consts)


def _attn_kernel(q_ref, k_ref, v_ref, o_ref):
    lane = lax.broadcasted_iota(jnp.int32, (1, LANES), 1)
    q_off = q_ref.shape[1] - LANES

    def q_tile(i, carry):
        rows = pl.ds(pl.multiple_of(i * Q_TILE, Q_TILE), Q_TILE)
        outs = []
        for j in range(2):
            q = q_ref[rows, j * q_off:j * q_off + LANES]
            k = k_ref[:, j * LANES:(j + 1) * LANES]
            v = v_ref[:, j * LANES:(j + 1) * LANES]
            s = lax.dot_general(q, k, (((1,), (1,)), ((), ())), preferred_element_type=F32)
            m = jnp.max(s, axis=-1, keepdims=True)
            p = jnp.exp2(s - m).astype(BF16)
            acc = jnp.dot(p, v, preferred_element_type=F32)
            one_lane = MLA_V if j == 0 else 0
            outs.append(acc / acc[:, one_lane:one_lane + 1])
        o_ref[rows, :] = jnp.where(lane < MLA_V, outs[0], outs[1]).astype(BF16)
        return carry

    lax.fori_loop(0, q_ref.shape[0] // Q_TILE, q_tile, 0)


def _attn_call(q, k, v, batch, seq, kv_share, name):
    pairs = 4
    qw = q.shape[1] // pairs
    kvw = 2 * LANES
    return pl.pallas_call(
        _attn_kernel,
        grid=(batch, pairs),
        in_specs=[
            pl.BlockSpec((seq, qw), lambda b, p: (b, p)),
            pl.BlockSpec((seq, kvw), lambda b, p: (b, p // kv_share)),
            pl.BlockSpec((seq, kvw), lambda b, p: (b, p // kv_share)),
        ],
        out_specs=pl.BlockSpec((seq, LANES), lambda b, p: (b, p)),
        out_shape=jax.ShapeDtypeStruct((q.shape[0], pairs * LANES), BF16),
        compiler_params=pltpu.CompilerParams(vmem_limit_bytes=VMEM_LIMIT_BYTES),
        name=name,
    )(q, k, v)


def _rope_tables(seq):
    rows = seq // GRID_W
    row = jnp.repeat(jnp.arange(rows, dtype=F32), GRID_W)
    col = jnp.tile(jnp.arange(GRID_W, dtype=F32), rows)

    def slot_tables(dim, lead, slot_w):
        half = dim // 2
        inv = ROPE_THETA ** (-jnp.arange(0, half, 2, dtype=F32) / half)
        ar, ac = row[:, None] * inv[None, :], col[:, None] * inv[None, :]
        z = jnp.zeros_like(ar)
        c = jnp.concatenate([jnp.cos(ar)] * 2 + [jnp.cos(ac)] * 2, axis=1)
        sa = jnp.concatenate([-jnp.sin(ar), z, -jnp.sin(ac), z], axis=1)
        sb = jnp.concatenate([z, jnp.sin(ar), z, jnp.sin(ac)], axis=1)
        tail = slot_w - lead - dim
        pad = lambda a, fill: jnp.concatenate(
            [jnp.full((seq, lead), fill, F32), a, jnp.full((seq, tail), fill, F32)], axis=1)
        return pad(c, 1.0), pad(sa, 0.0), pad(sb, 0.0)

    mla = slot_tables(MLA_ROPE, MLA_NOPE, LANES)
    gqa = [jnp.tile(a, (1, LANES // GQA_HD)) for a in slot_tables(GQA_HD, 0, GQA_HD)]
    return jnp.concatenate(list(mla) + gqa, axis=1)


def _layer_params(l, norm_ffn1, w_ffn1_gate, w_ffn1_up, w_ffn1_down, norm_mix, w_in, q_a_norm, w_q_b,
                  kv_a_norm, w_kv_b, gqa_q_norm, gqa_k_norm, w_out, norm_ffn2, w_ffn2_gate, w_ffn2_up,
                  w_ffn2_down):
    nch = D_FF // FF_CHUNK

    def gate_up(wg, wu):
        w = jnp.stack([wg.reshape(D_MODEL, nch, FF_CHUNK), wu.reshape(D_MODEL, nch, FF_CHUNK)], axis=2)
        return w.reshape(D_MODEL, 2 * D_FF).astype(BF16)

    row = lambda v: v.reshape(1, -1).astype(F32)
    wi = w_in[l]
    o = np.cumsum([0, Q_LORA, KV_LORA, MLA_ROPE, GQA_HEADS * GQA_HD, GQA_KV_HEADS * GQA_HD, GQA_KV_HEADS * GQA_HD])
    zc = lambda n: jnp.zeros((D_MODEL, n), F32)
    win = jnp.concatenate(
        [wi[:, o[0]:o[1]], wi[:, o[1]:o[2]], zc(MLA_NOPE), wi[:, o[2]:o[3]], zc(LANES - MLA_NOPE - MLA_ROPE),
         wi[:, o[3]:o[4]], wi[:, o[4]:o[5]], wi[:, o[5]:o[6]]], axis=1).astype(BF16)
    assert win.shape[1] == _ZW

    wqb = w_q_b[l].reshape(Q_LORA, MLA_HEADS, MLA_NOPE + MLA_ROPE)
    wqb = jnp.pad(wqb, ((0, 0), (0, 0), (0, LANES - MLA_NOPE - MLA_ROPE))).reshape(Q_LORA, MLA_HEADS * LANES)

    wkv = w_kv_b[l].reshape(KV_LORA, MLA_HEADS, MLA_NOPE + MLA_V)
    wk = jnp.pad(wkv[:, :, :MLA_NOPE], ((0, 0), (0, 0), (0, LANES - MLA_NOPE)))
    wv_a = jnp.pad(wkv[:, :, MLA_NOPE:], ((0, 0), (0, 0), (0, LANES - MLA_V)))
    wv_b = jnp.pad(wkv[:, :, MLA_NOPE:], ((0, 0), (0, 0), (LANES - MLA_V, 0)))
    odd = (jnp.arange(MLA_HEADS) % 2 == 1)[None, :, None]
    wkvb = jnp.concatenate([wk, jnp.where(odd, wv_b, wv_a)], axis=1).reshape(KV_LORA, 2 * MLA_HEADS * LANES)

    gmat = np.kron(np.eye(GQA_HEADS), np.full((GQA_HD, GQA_HD), 1.0 / GQA_HD))
    return dict(
        g1=row(norm_ffn1[l]), wgu1=gate_up(w_ffn1_gate[l], w_ffn1_up[l]), wd1=w_ffn1_down[l].astype(BF16),
        gmix=row(norm_mix[l]), win=win, gqa=row(q_a_norm[l]), wqb=wqb.astype(BF16), gkv=row(kv_a_norm[l]),
        wkvb=wkvb.astype(BF16),
        gq=row(jnp.tile(gqa_q_norm[l], GQA_HEADS)) * (GQA_HD ** -0.5 * LOG2E),
        gk=row(jnp.tile(gqa_k_norm[l], GQA_KV_HEADS)),
        gmat=jnp.asarray(gmat, BF16), wout=w_out[l].astype(BF16),
        g2=row(norm_ffn2[l]), wgu2=gate_up(w_ffn2_gate[l], w_ffn2_up[l]), wd2=w_ffn2_down[l].astype(BF16),
    )


def _trunk(x3, layers, tab, final_g):
    batch, seq, _ = x3.shape
    assert seq % ROW_TILE == 0 and seq % Q_TILE == 0 and seq % GRID_W == 0
    x = x3.reshape(batch * seq, D_MODEL)
    for li, p in enumerate(layers):
        x = _ffn_call(x, p["g1"], p["wgu1"], p["wd1"])
        qm, km, vm, qg, kg, vg = _prep_call(x, tab, seq, p)
        o_mla = _attn_call(qm, km, vm, batch, seq, 1, "attn_mla")
        o_gqa = _attn_call(qg, kg, vg, batch, seq, 2, "attn_gqa")
        x = _ffn_call(x, p["g2"], p["wgu2"], p["wd2"], attn=(o_mla, o_gqa, p["wout"]),
                      final_g=final_g if li == len(layers) - 1 else None)
    return x.reshape(batch, seq, D_MODEL)


def kernel(x_prompt, x_sample, norm_ffn1, w_ffn1_gate, w_ffn1_up, w_ffn1_down, norm_mix, w_in, q_a_norm, w_q_b, kv_a_norm, w_kv_b, gqa_q_norm, gqa_k_norm, w_out, norm_ffn2, w_ffn2_gate, w_ffn2_up, w_ffn2_down, final_norm):
    depth = norm_ffn1.shape[0]
    layers = [
        _layer_params(l, norm_ffn1, w_ffn1_gate, w_ffn1_up, w_ffn1_down, norm_mix, w_in, q_a_norm, w_q_b,
                      kv_a_norm, w_kv_b, gqa_q_norm, gqa_k_norm, w_out, norm_ffn2, w_ffn2_gate, w_ffn2_up,
                      w_ffn2_down)
        for l in range(depth)
    ]
    final_g = final_norm.reshape(1, -1).astype(F32)
    outs = []
    for x3 in (x_prompt, x_sample):
        tab = _rope_tables(x3.shape[1])
        outs.append(_trunk(x3, layers, tab, final_g))
    return tuple(outs)
```

```python
import functools

import jax
import jax.numpy as jnp
import numpy as np
from jax import lax
from jax.experimental import pallas as pl
from jax.experimental.pallas import tpu as pltpu

D_MODEL = 1024
GRID_W = 64
ROPE_THETA = 10000.0
EPS = 1e-6

MLA_HEADS = 8
MLA_NOPE = 64
MLA_ROPE = 32
MLA_V = 64
Q_LORA = 256
KV_LORA = 128
GQA_HEADS = 8
GQA_KV_HEADS = 2
GQA_HD = 64
D_FF = 2816

LANES = 128
LOG2E = 1.4426950408889634

ROW_TILE = 512
FF_CHUNK = 256
Q_TILE = 512
KEY_CHUNK = 512
MAX_PARTS = 4
CHUNK_UNROLL = 4
VMEM_LIMIT_BYTES = 56 * 1024 * 1024

_ZW = Q_LORA + KV_LORA + LANES + GQA_HEADS * GQA_HD + 2 * LANES
_Z_CQ = (0, Q_LORA)
_Z_CKV = (Q_LORA, Q_LORA + KV_LORA)
_Z_KPE = (_Z_CKV[1], _Z_CKV[1] + LANES)
_Z_QG = (_Z_KPE[1], _Z_KPE[1] + GQA_HEADS * GQA_HD)
_Z_KG = (_Z_QG[1], _Z_QG[1] + LANES)
_Z_VG = (_Z_KG[1], _Z_KG[1] + LANES)

BF16 = jnp.bfloat16
F32 = jnp.float32


def _rms(x, g):
    ms = jnp.mean(x * x, axis=-1, keepdims=True)
    return x * lax.rsqrt(ms + EPS) * g


def _rope(x, c, sa, sb, d):
    return x * c + pltpu.roll(x, LANES - d, 1) * sa + pltpu.roll(x, d, 1) * sb


def _ffn_kernel(*refs, has_attn, final):
    it = iter(refs)
    x_ref = next(it)
    if has_attn:
        om_ref, og_ref, wo_ref = next(it), next(it), next(it)
    g_ref, wgu_ref, wd_ref = next(it), next(it), next(it)
    if final:
        fg_ref = next(it)
    o_ref = next(it)
    a_ref = next(it)

    x = x_ref[...]
    if has_attn:
        half = om_ref.shape[1]
        x = x + jnp.dot(om_ref[...], wo_ref[:half, :], preferred_element_type=F32)
        x = x + jnp.dot(og_ref[...], wo_ref[half:, :], preferred_element_type=F32)
    hb = _rms(x, g_ref[...]).astype(BF16)
    for c in range(D_FF // FF_CHUNK):
        gu = jnp.dot(hb, wgu_ref[:, c * 2 * FF_CHUNK:(c + 1) * 2 * FF_CHUNK], preferred_element_type=F32)
        g = gu[:, :FF_CHUNK]
        u = gu[:, FF_CHUNK:]
        a_ref[:, c * FF_CHUNK:(c + 1) * FF_CHUNK] = (g * jax.nn.sigmoid(g) * u).astype(BF16)
    y = x + 0.5 * jnp.dot(a_ref[...], wd_ref[...], preferred_element_type=F32)
    if final:
        y = _rms(y, fg_ref[...])
    o_ref[...] = y


def _const_spec(shape):
    return pl.BlockSpec(shape, lambda *_: (0,) * len(shape), pipeline_mode=pl.Buffered(1))


def _ffn_call(x, g, wgu, wd, attn=None, final_g=None):
    t = x.shape[0]
    row = lambda w: pl.BlockSpec((ROW_TILE, w), lambda i: (i, 0))
    args, specs = [x], [row(D_MODEL)]
    if attn is not None:
        o_mla, o_gqa, w_out = attn
        args += [o_mla, o_gqa, w_out]
        specs += [row(o_mla.shape[1]), row(o_gqa.shape[1]), _const_spec(w_out.shape)]
    args += [g, wgu, wd]
    specs += [_const_spec(g.shape), _const_spec(wgu.shape), _const_spec(wd.shape)]
    if final_g is not None:
        args.append(final_g)
        specs.append(_const_spec(final_g.shape))
    return pl.pallas_call(
        functools.partial(_ffn_kernel, has_attn=attn is not None, final=final_g is not None),
        grid=(t // ROW_TILE,),
        in_specs=specs,
        out_specs=row(D_MODEL),
        out_shape=jax.ShapeDtypeStruct((t, D_MODEL), F32),
        scratch_shapes=[pltpu.VMEM((ROW_TILE, D_FF), BF16)],
        compiler_params=pltpu.CompilerParams(vmem_limit_bytes=VMEM_LIMIT_BYTES),
        name="ffn_attn" if attn is not None else "ffn",
    )(*args)


def _prep_kernel(x_ref, tab_ref, gmix_ref, win_ref, gqa_ref, wqb_ref, gkv_ref, wkvb_ref, gq_ref, gk_ref,
                 gmat_ref, qm_ref, km_ref, vm_ref, qg_ref, kg_ref, vg_ref):
    hb = _rms(x_ref[...], gmix_ref[...]).astype(BF16)
    z = jnp.dot(hb, win_ref[...], preferred_element_type=F32)

    mc, msa, msb = (tab_ref[:, j * LANES:(j + 1) * LANES] for j in range(3))
    gc, gsa, gsb = (tab_ref[:, j * LANES:(j + 1) * LANES] for j in range(3, 6))
    lane = lax.broadcasted_iota(jnp.int32, (1, LANES), 1)
    lo = lane < GQA_HD
    one_a = (lane == MLA_V).astype(F32)
    one_b = (lane == 0).astype(F32)

    cq = _rms(z[:, _Z_CQ[0]:_Z_CQ[1]], gqa_ref[...]).astype(BF16)
    qa = jnp.dot(cq, wqb_ref[...], preferred_element_type=F32)
    q_scale = (MLA_NOPE + MLA_ROPE) ** -0.5 * LOG2E
    for h in range(MLA_HEADS):
        sl = slice(h * LANES, (h + 1) * LANES)
        qm_ref[:, sl] = (_rope(qa[:, sl], mc, msa, msb, MLA_ROPE // 4) * q_scale).astype(BF16)

    ckv = _rms(z[:, _Z_CKV[0]:_Z_CKV[1]], gkv_ref[...]).astype(BF16)
    kv = jnp.dot(ckv, wkvb_ref[...], preferred_element_type=F32)
    kpe = _rope(z[:, _Z_KPE[0]:_Z_KPE[1]], mc, msa, msb, MLA_ROPE // 4)
    for h in range(MLA_HEADS):
        sl = slice(h * LANES, (h + 1) * LANES)
        km_ref[:, sl] = (kv[:, sl] + kpe).astype(BF16)
        vsl = slice((MLA_HEADS + h) * LANES, (MLA_HEADS + h + 1) * LANES)
        vm_ref[:, sl] = (kv[:, vsl] + (one_a if h % 2 == 0 else one_b)).astype(BF16)

    def head_mean_sq(v, gmat):
        sq = v * v
        hi = sq.astype(BF16)
        lo_part = (sq - hi.astype(F32)).astype(BF16)
        return (jnp.dot(hi, gmat, preferred_element_type=F32)
                + jnp.dot(lo_part, gmat, preferred_element_type=F32))

    qg = z[:, _Z_QG[0]:_Z_QG[1]]
    qg = qg * lax.rsqrt(head_mean_sq(qg, gmat_ref[...]) + EPS) * gq_ref[...]
    for j in range(GQA_HEADS // 2):
        sl = slice(j * LANES, (j + 1) * LANES)
        qg_ref[:, sl] = _rope(qg[:, sl], gc, gsa, gsb, GQA_HD // 4).astype(BF16)

    kg = z[:, _Z_KG[0]:_Z_KG[1]]
    kg = kg * lax.rsqrt(head_mean_sq(kg, gmat_ref[:LANES, :LANES]) + EPS) * gk_ref[...]
    kg = _rope(kg, gc, gsa, gsb, GQA_HD // 4)
    kg_sw = pltpu.roll(kg, GQA_HD, 1)
    zero = jnp.zeros_like(kg)
    kg_ref[:, 0 * LANES:1 * LANES] = jnp.where(lo, kg, zero).astype(BF16)
    kg_ref[:, 1 * LANES:2 * LANES] = jnp.where(lo, zero, kg_sw).astype(BF16)
    kg_ref[:, 2 * LANES:3 * LANES] = jnp.where(lo, kg_sw, zero).astype(BF16)
    kg_ref[:, 3 * LANES:4 * LANES] = jnp.where(lo, zero, kg).astype(BF16)

    vg = z[:, _Z_VG[0]:_Z_VG[1]]
    vg_sw = pltpu.roll(vg, GQA_HD, 1)
    vg_ref[:, 0 * LANES:1 * LANES] = jnp.where(lo, vg, one_a).astype(BF16)
    vg_ref[:, 1 * LANES:2 * LANES] = jnp.where(lo, one_b, vg_sw).astype(BF16)
    vg_ref[:, 2 * LANES:3 * LANES] = jnp.where(lo, vg_sw, one_a).astype(BF16)
    vg_ref[:, 3 * LANES:4 * LANES] = jnp.where(lo, one_b, vg).astype(BF16)


def _prep_call(x, tab, seq, p):
    t = x.shape[0]
    tiles_per_seq = seq // ROW_TILE
    row = lambda w: pl.BlockSpec((ROW_TILE, w), lambda i: (i, 0))
    consts = [p["gmix"], p["win"], p["gqa"], p["wqb"], p["gkv"], p["wkvb"], p["gq"], p["gk"], p["gmat"]]
    widths = (MLA_HEADS * LANES,) * 3 + (GQA_HEADS // 2 * LANES,) * 3
    return pl.pallas_call(
        _prep_kernel,
        grid=(t // ROW_TILE,),
        in_specs=[row(D_MODEL), pl.BlockSpec((ROW_TILE, tab.shape[1]), lambda i: (i % tiles_per_seq, 0))]
        + [_const_spec(c.shape) for c in consts],
        out_specs=[row(w) for w in widths],
        out_shape=[jax.ShapeDtypeStruct((t, w), BF16) for w in widths],
        compiler_params=pltpu.CompilerParams(vmem_limit_bytes=VMEM_LIMIT_BYTES),
        name="prep",
    )(x, tab, *consts)


def _attn_kernel(q_ref, k_ref, v_ref, o_ref, vt_ref, sa_ref, sb_ref, acc_ref):
    seq = k_ref.shape[0]
    nq, nc = seq // Q_TILE, seq // KEY_CHUNK
    assert nq % 2 == 0
    q_off = q_ref.shape[1] - LANES
    cols = [slice(j * LANES, (j + 1) * LANES) for j in range(2)]
    for j in range(2):
        for c in range(nc):
            vt_ref[j, c] = v_ref[c * KEY_CHUNK:(c + 1) * KEY_CHUNK, cols[j]].T
    row = lax.broadcasted_iota(jnp.int32, (LANES, 1), 0)
    neg_inf = [jnp.full((MAX_PARTS * 8, Q_TILE), -jnp.inf, F32)] * 2

    def score_chunk(i, c, s_ref, mparts):
        qrows = pl.ds(pl.multiple_of(i * Q_TILE, Q_TILE), Q_TILE)
        krows = pl.ds(pl.multiple_of(c * KEY_CHUNK, KEY_CHUNK), KEY_CHUNK)
        out = []
        for j in range(2):
            q = q_ref[qrows, j * q_off:j * q_off + LANES]
            st = lax.dot_general(k_ref[krows, cols[j]], q, (((1,), (1,)), ((), ())),
                                 preferred_element_type=F32)
            s_ref[j, krows, :] = st
            part = jnp.max(st.reshape(KEY_CHUNK // (MAX_PARTS * 8), MAX_PARTS * 8, Q_TILE), axis=0)
            out.append(jnp.maximum(mparts[j], part))
        return out

    def pv_chunk(c, s_ref, m):
        krows = pl.ds(pl.multiple_of(c * KEY_CHUNK, KEY_CHUNK), KEY_CHUNK)
        for j in range(2):
            pt = jnp.exp2((s_ref[j, krows, :] - m[j]).astype(BF16))
            acc_ref[j] += jnp.dot(vt_ref[j, c], pt, preferred_element_type=F32)

    def col_max(mparts):
        return [jnp.max(mp, axis=0, keepdims=True) for mp in mparts]

    def finish(i):
        a0, a1 = acc_ref[0], acc_ref[1]
        ot = jnp.where(row < MLA_V, a0 / a0[MLA_V:MLA_V + 1, :], a1 / a1[0:1, :])
        o_ref[pl.ds(pl.multiple_of(i * Q_TILE, Q_TILE), Q_TILE), :] = ot.T.astype(BF16)
        acc_ref[...] = jnp.zeros_like(acc_ref)

    def tile_step(i, m_prev, s_cur, s_prev):
        def chunk(c, mp):
            mp = score_chunk(i, c, s_cur, mp)
            pv_chunk(c, s_prev, m_prev)
            return mp

        mparts = lax.fori_loop(0, nc, chunk, neg_inf, unroll=CHUNK_UNROLL)
        finish(i - 1)
        return col_max(mparts)

    acc_ref[...] = jnp.zeros_like(acc_ref)
    m = col_max(lax.fori_loop(0, nc, lambda c, mp: score_chunk(0, c, sa_ref, mp), neg_inf))

    def tile_pair(t, m):
        m = tile_step(2 * t + 1, m, sb_ref, sa_ref)
        return tile_step(2 * t + 2, m, sa_ref, sb_ref)

    m = lax.fori_loop(0, nq // 2 - 1, tile_pair, m)
    m = tile_step(nq - 1, m, sb_ref, sa_ref)

    def last_chunk(c, carry):
        pv_chunk(c, sb_ref, m)
        return carry

    lax.fori_loop(0, nc, last_chunk, 0)
    finish(nq - 1)


def _attn_call(q, k, v, batch, seq, kv_share, name):
    pairs = 4
    qw = q.shape[1] // pairs
    kvw = 2 * LANES
    return pl.pallas_call(
        _attn_kernel,
        grid=(batch, pairs),
        in_specs=[
            pl.BlockSpec((seq, qw), lambda b, p: (b, p)),
            pl.BlockSpec((seq, kvw), lambda b, p: (b, p // kv_share)),
            pl.BlockSpec((seq, kvw), lambda b, p: (b, p // kv_share)),
        ],
        out_specs=pl.BlockSpec((seq, LANES), lambda b, p: (b, p)),
        out_shape=jax.ShapeDtypeStruct((q.shape[0], pairs * LANES), BF16),
        scratch_shapes=[
            pltpu.VMEM((2, seq // KEY_CHUNK, LANES, KEY_CHUNK), BF16),
            pltpu.VMEM((2, seq, Q_TILE), F32),
            pltpu.VMEM((2, seq, Q_TILE), F32),
            pltpu.VMEM((2, LANES, Q_TILE), F32),
        ],
        compiler_params=pltpu.CompilerParams(vmem_limit_bytes=VMEM_LIMIT_BYTES),
        name=name,
    )(q, k, v)


def _rope_tables(seq):
    rows = seq // GRID_W
    row = jnp.repeat(jnp.arange(rows, dtype=F32), GRID_W)
    col = jnp.tile(jnp.arange(GRID_W, dtype=F32), rows)

    def slot_tables(dim, lead, slot_w):
        half = dim // 2
        inv = ROPE_THETA ** (-jnp.arange(0, half, 2, dtype=F32) / half)
        ar, ac = row[:, None] * inv[None, :], col[:, None] * inv[None, :]
        z = jnp.zeros_like(ar)
        c = jnp.concatenate([jnp.cos(ar)] * 2 + [jnp.cos(ac)] * 2, axis=1)
        sa = jnp.concatenate([-jnp.sin(ar), z, -jnp.sin(ac), z], axis=1)
        sb = jnp.concatenate([z, jnp.sin(ar), z, jnp.sin(ac)], axis=1)
        tail = slot_w - lead - dim
        pad = lambda a, fill: jnp.concatenate(
            [jnp.full((seq, lead), fill, F32), a, jnp.full((seq, tail), fill, F32)], axis=1)
        return pad(c, 1.0), pad(sa, 0.0), pad(sb, 0.0)

    mla = slot_tables(MLA_ROPE, MLA_NOPE, LANES)
    gqa = [jnp.tile(a, (1, LANES // GQA_HD)) for a in slot_tables(GQA_HD, 0, GQA_HD)]
    return jnp.concatenate(list(mla) + gqa, axis=1)


def _layer_params(l, norm_ffn1, w_ffn1_gate, w_ffn1_up, w_ffn1_down, norm_mix, w_in, q_a_norm, w_q_b,
                  kv_a_norm, w_kv_b, gqa_q_norm, gqa_k_norm, w_out, norm_ffn2, w_ffn2_gate, w_ffn2_up,
                  w_ffn2_down):
    nch = D_FF // FF_CHUNK

    def gate_up(wg, wu):
        w = jnp.stack([wg.reshape(D_MODEL, nch, FF_CHUNK), wu.reshape(D_MODEL, nch, FF_CHUNK)], axis=2)
        return w.reshape(D_MODEL, 2 * D_FF).astype(BF16)

    row = lambda v: v.reshape(1, -1).astype(F32)
    wi = w_in[l]
    o = np.cumsum([0, Q_LORA, KV_LORA, MLA_ROPE, GQA_HEADS * GQA_HD, GQA_KV_HEADS * GQA_HD, GQA_KV_HEADS * GQA_HD])
    zc = lambda n: jnp.zeros((D_MODEL, n), F32)
    win = jnp.concatenate(
        [wi[:, o[0]:o[1]], wi[:, o[1]:o[2]], zc(MLA_NOPE), wi[:, o[2]:o[3]], zc(LANES - MLA_NOPE - MLA_ROPE),
         wi[:, o[3]:o[4]], wi[:, o[4]:o[5]], wi[:, o[5]:o[6]]], axis=1).astype(BF16)
    assert win.shape[1] == _ZW

    wqb = w_q_b[l].reshape(Q_LORA, MLA_HEADS, MLA_NOPE + MLA_ROPE)
    wqb = jnp.pad(wqb, ((0, 0), (0, 0), (0, LANES - MLA_NOPE - MLA_ROPE))).reshape(Q_LORA, MLA_HEADS * LANES)

    wkv = w_kv_b[l].reshape(KV_LORA, MLA_HEADS, MLA_NOPE + MLA_V)
    wk = jnp.pad(wkv[:, :, :MLA_NOPE], ((0, 0), (0, 0), (0, LANES - MLA_NOPE)))
    wv_a = jnp.pad(wkv[:, :, MLA_NOPE:], ((0, 0), (0, 0), (0, LANES - MLA_V)))
    wv_b = jnp.pad(wkv[:, :, MLA_NOPE:], ((0, 0), (0, 0), (LANES - MLA_V, 0)))
    odd = (jnp.arange(MLA_HEADS) % 2 == 1)[None, :, None]
    wkvb = jnp.concatenate([wk, jnp.where(odd, wv_b, wv_a)], axis=1).reshape(KV_LORA, 2 * MLA_HEADS * LANES)

    gmat = np.kron(np.eye(GQA_HEADS), np.full((GQA_HD, GQA_HD), 1.0 / GQA_HD))
    return dict(
        g1=row(norm_ffn1[l]), wgu1=gate_up(w_ffn1_gate[l], w_ffn1_up[l]), wd1=w_ffn1_down[l].astype(BF16),
        gmix=row(norm_mix[l]), win=win, gqa=row(q_a_norm[l]), wqb=wqb.astype(BF16), gkv=row(kv_a_norm[l]),
        wkvb=wkvb.astype(BF16),
        gq=row(jnp.tile(gqa_q_norm[l], GQA_HEADS)) * (GQA_HD ** -0.5 * LOG2E),
        gk=row(jnp.tile(gqa_k_norm[l], GQA_KV_HEADS)),
        gmat=jnp.asarray(gmat, BF16), wout=w_out[l].astype(BF16),
        g2=row(norm_ffn2[l]), wgu2=gate_up(w_ffn2_gate[l], w_ffn2_up[l]), wd2=w_ffn2_down[l].astype(BF16),
    )


def _trunk(x3, layers, tab, final_g):
    batch, seq, _ = x3.shape
    assert seq % ROW_TILE == 0 and seq % Q_TILE == 0 and seq % GRID_W == 0
    x = x3.reshape(batch * seq, D_MODEL)
    for li, p in enumerate(layers):
        x = _ffn_call(x, p["g1"], p["wgu1"], p["wd1"])
        qm, km, vm, qg, kg, vg = _prep_call(x, tab, seq, p)
        o_mla = _attn_call(qm, km, vm, batch, seq, 1, "attn_mla")
        o_gqa = _attn_call(qg, kg, vg, batch, seq, 2, "attn_gqa")
        x = _ffn_call(x, p["g2"], p["wgu2"], p["wd2"], attn=(o_mla, o_gqa, p["wout"]),
                      final_g=final_g if li == len(layers) - 1 else None)
    return x.reshape(batch, seq, D_MODEL)


def kernel(x_prompt, x_sample, norm_ffn1, w_ffn1_gate, w_ffn1_up, w_ffn1_down, norm_mix, w_in, q_a_norm, w_q_b, kv_a_norm, w_kv_b, gqa_q_norm, gqa_k_norm, w_out, norm_ffn2, w_ffn2_gate, w_ffn2_up, w_ffn2_down, final_norm):
    depth = norm_ffn1.shape[0]
    layers = [
        _layer_params(l, norm_ffn1, w_ffn1_gate, w_ffn1_up, w_ffn1_down, norm_mix, w_in, q_a_norm, w_q_b,
                      kv_a_norm, w_kv_b, gqa_q_norm, gqa_k_norm, w_out, norm_ffn2, w_ffn2_gate, w_ffn2_up,
                      w_ffn2_down)
        for l in range(depth)
    ]
    final_g = final_norm.reshape(1, -1).astype(F32)
    outs = []
    for x3 in (x_prompt, x_sample):
        tab = _rope_tables(x3.shape[1])
        outs.append(_trunk(x3, layers, tab, final_g))
    return tuple(outs)
```

```python
import functools

import jax
import jax.numpy as jnp
import numpy as np
from jax import lax
from jax.experimental import pallas as pl
from jax.experimental.pallas import tpu as pltpu

D_MODEL = 1024
GRID_W = 64
ROPE_THETA = 10000.0
EPS = 1e-6

MLA_HEADS = 8
MLA_NOPE = 64
MLA_ROPE = 32
MLA_V = 64
Q_LORA = 256
KV_LORA = 128
GQA_HEADS = 8
GQA_KV_HEADS = 2
GQA_HD = 64
D_FF = 2816

LANES = 128
LOG2E = 1.4426950408889634

ROW_TILE = 512
FF_CHUNK = 256
Q_TILE = 512
KEY_CHUNK = 512
PV_ROWS = 80
MAX_ROWS = 32
CHUNK_UNROLL = 4
VMEM_LIMIT_BYTES = 56 * 1024 * 1024

_ZW = Q_LORA + KV_LORA + LANES + GQA_HEADS * GQA_HD + 2 * LANES
_Z_CQ = (0, Q_LORA)
_Z_CKV = (Q_LORA, Q_LORA + KV_LORA)
_Z_KPE = (_Z_CKV[1], _Z_CKV[1] + LANES)
_Z_QG = (_Z_KPE[1], _Z_KPE[1] + GQA_HEADS * GQA_HD)
_Z_KG = (_Z_QG[1], _Z_QG[1] + LANES)
_Z_VG = (_Z_KG[1], _Z_KG[1] + LANES)

BF16 = jnp.bfloat16
F32 = jnp.float32


def _rms(x, g):
    ms = jnp.mean(x * x, axis=-1, keepdims=True)
    return x * lax.rsqrt(ms + EPS) * g


def _rope(x, c, sa, sb, d):
    return x * c + pltpu.roll(x, LANES - d, 1) * sa + pltpu.roll(x, d, 1) * sb


def _ffn_kernel(*refs, has_attn, final):
    it = iter(refs)
    x_ref = next(it)
    if has_attn:
        om_ref, og_ref, wo_ref = next(it), next(it), next(it)
    g_ref, wgu_ref, wd_ref = next(it), next(it), next(it)
    if final:
        fg_ref = next(it)
    o_ref = next(it)
    a_ref = next(it)

    x = x_ref[...]
    if has_attn:
        half = om_ref.shape[1]
        x = x + jnp.dot(om_ref[...], wo_ref[:half, :], preferred_element_type=F32)
        x = x + jnp.dot(og_ref[...], wo_ref[half:, :], preferred_element_type=F32)
    hb = _rms(x, g_ref[...]).astype(BF16)
    for c in range(D_FF // FF_CHUNK):
        gu = jnp.dot(hb, wgu_ref[:, c * 2 * FF_CHUNK:(c + 1) * 2 * FF_CHUNK], preferred_element_type=F32)
        g = gu[:, :FF_CHUNK]
        u = gu[:, FF_CHUNK:]
        a_ref[:, c * FF_CHUNK:(c + 1) * FF_CHUNK] = (g * jax.nn.sigmoid(g) * u).astype(BF16)
    y = x + 0.5 * jnp.dot(a_ref[...], wd_ref[...], preferred_element_type=F32)
    if final:
        y = _rms(y, fg_ref[...])
    o_ref[...] = y


def _const_spec(shape):
    return pl.BlockSpec(shape, lambda *_: (0,) * len(shape), pipeline_mode=pl.Buffered(1))


def _ffn_call(x, g, wgu, wd, attn=None, final_g=None):
    t = x.shape[0]
    row = lambda w: pl.BlockSpec((ROW_TILE, w), lambda i: (i, 0))
    args, specs = [x], [row(D_MODEL)]
    if attn is not None:
        o_mla, o_gqa, w_out = attn
        args += [o_mla, o_gqa, w_out]
        specs += [row(o_mla.shape[1]), row(o_gqa.shape[1]), _const_spec(w_out.shape)]
    args += [g, wgu, wd]
    specs += [_const_spec(g.shape), _const_spec(wgu.shape), _const_spec(wd.shape)]
    if final_g is not None:
        args.append(final_g)
        specs.append(_const_spec(final_g.shape))
    return pl.pallas_call(
        functools.partial(_ffn_kernel, has_attn=attn is not None, final=final_g is not None),
        grid=(t // ROW_TILE,),
        in_specs=specs,
        out_specs=row(D_MODEL),
        out_shape=jax.ShapeDtypeStruct((t, D_MODEL), F32),
        scratch_shapes=[pltpu.VMEM((ROW_TILE, D_FF), BF16)],
        compiler_params=pltpu.CompilerParams(vmem_limit_bytes=VMEM_LIMIT_BYTES),
        name="ffn_attn" if attn is not None else "ffn",
    )(*args)


def _prep_kernel(x_ref, tab_ref, gmix_ref, win_ref, gqa_ref, wqb_ref, gkv_ref, wkvb_ref, gq_ref, gk_ref,
                 gmat_ref, qm_ref, km_ref, vm_ref, qg_ref, kg_ref, vg_ref):
    hb = _rms(x_ref[...], gmix_ref[...]).astype(BF16)
    z = jnp.dot(hb, win_ref[...], preferred_element_type=F32)

    mc, msa, msb = (tab_ref[:, j * LANES:(j + 1) * LANES] for j in range(3))
    gc, gsa, gsb = (tab_ref[:, j * LANES:(j + 1) * LANES] for j in range(3, 6))
    lane = lax.broadcasted_iota(jnp.int32, (1, LANES), 1)
    lo = lane < GQA_HD
    one = (lane == MLA_V).astype(F32)

    cq = _rms(z[:, _Z_CQ[0]:_Z_CQ[1]], gqa_ref[...]).astype(BF16)
    qa = jnp.dot(cq, wqb_ref[...], preferred_element_type=F32)
    q_scale = (MLA_NOPE + MLA_ROPE) ** -0.5 * LOG2E
    for h in range(MLA_HEADS):
        sl = slice(h * LANES, (h + 1) * LANES)
        qm_ref[:, sl] = (_rope(qa[:, sl], mc, msa, msb, MLA_ROPE // 4) * q_scale).astype(BF16)

    ckv = _rms(z[:, _Z_CKV[0]:_Z_CKV[1]], gkv_ref[...]).astype(BF16)
    kv = jnp.dot(ckv, wkvb_ref[...], preferred_element_type=F32)
    kpe = _rope(z[:, _Z_KPE[0]:_Z_KPE[1]], mc, msa, msb, MLA_ROPE // 4)
    for h in range(MLA_HEADS):
        sl = slice(h * LANES, (h + 1) * LANES)
        km_ref[:, sl] = (kv[:, sl] + kpe).astype(BF16)
        vsl = slice((MLA_HEADS + h) * LANES, (MLA_HEADS + h + 1) * LANES)
        vm_ref[:, sl] = (kv[:, vsl] + one).astype(BF16)

    def head_mean_sq(v, gmat):
        sq = v * v
        hi = sq.astype(BF16)
        lo_part = (sq - hi.astype(F32)).astype(BF16)
        return (jnp.dot(hi, gmat, preferred_element_type=F32)
                + jnp.dot(lo_part, gmat, preferred_element_type=F32))

    qg = z[:, _Z_QG[0]:_Z_QG[1]]
    qg = qg * lax.rsqrt(head_mean_sq(qg, gmat_ref[...]) + EPS) * gq_ref[...]
    for j in range(GQA_HEADS // 2):
        sl = slice(j * LANES, (j + 1) * LANES)
        qg_ref[:, sl] = _rope(qg[:, sl], gc, gsa, gsb, GQA_HD // 4).astype(BF16)

    kg = z[:, _Z_KG[0]:_Z_KG[1]]
    kg = kg * lax.rsqrt(head_mean_sq(kg, gmat_ref[:LANES, :LANES]) + EPS) * gk_ref[...]
    kg = _rope(kg, gc, gsa, gsb, GQA_HD // 4)
    kg_sw = pltpu.roll(kg, GQA_HD, 1)
    zero = jnp.zeros_like(kg)
    kg_ref[:, 0 * LANES:1 * LANES] = jnp.where(lo, kg, zero).astype(BF16)
    kg_ref[:, 1 * LANES:2 * LANES] = jnp.where(lo, zero, kg_sw).astype(BF16)
    kg_ref[:, 2 * LANES:3 * LANES] = jnp.where(lo, kg_sw, zero).astype(BF16)
    kg_ref[:, 3 * LANES:4 * LANES] = jnp.where(lo, zero, kg).astype(BF16)

    vg = z[:, _Z_VG[0]:_Z_VG[1]]
    vg_sw = pltpu.roll(vg, GQA_HD, 1)
    vg_ref[:, 0 * LANES:1 * LANES] = jnp.where(lo, vg, one).astype(BF16)
    vg_ref[:, 1 * LANES:2 * LANES] = jnp.where(lo, vg_sw, one).astype(BF16)


def _prep_call(x, tab, seq, p):
    t = x.shape[0]
    tiles_per_seq = seq // ROW_TILE
    row = lambda w: pl.BlockSpec((ROW_TILE, w), lambda i: (i, 0))
    consts = [p["gmix"], p["win"], p["gqa"], p["wqb"], p["gkv"], p["wkvb"], p["gq"], p["gk"], p["gmat"]]
    widths = (MLA_HEADS * LANES,) * 3 + (GQA_HEADS // 2 * LANES, 2 * GQA_KV_HEADS * LANES, GQA_KV_HEADS * LANES)
    return pl.pallas_call(
        _prep_kernel,
        grid=(t // ROW_TILE,),
        in_specs=[row(D_MODEL), pl.BlockSpec((ROW_TILE, tab.shape[1]), lambda i: (i % tiles_per_seq, 0))]
        + [_const_spec(c.shape) for c in consts],
        out_specs=[row(w) for w in widths],
        out_shape=[jax.ShapeDtypeStruct((t, w), BF16) for w in widths],
        compiler_params=pltpu.CompilerParams(vmem_limit_bytes=VMEM_LIMIT_BYTES),
        name="prep",
    )(x, tab, *consts)


def _attn_kernel(q_ref, k_ref, v_ref, o_ref, vt_ref, sa_ref, sb_ref, acc_ref, *, q_slots, k_slots, v_slots):
    seq = k_ref.shape[0]
    nq, nc, npairs = seq // Q_TILE, seq // KEY_CHUNK, len(q_slots) // 2
    tiles_per_iter = 2 if npairs % 2 else 1
    items_per_iter = npairs * tiles_per_iter
    n_iter = nq // tiles_per_iter
    assert len(q_slots) % 2 == 0 and nq % tiles_per_iter == 0
    s_bufs = (sa_ref, sb_ref)
    for slot in sorted(set(v_slots)):
        for c in range(nc):
            vt = v_ref[c * KEY_CHUNK:(c + 1) * KEY_CHUNK, slot * LANES:(slot + 1) * LANES].T
            vt_ref[slot, c] = vt[:PV_ROWS, :]
    neg_inf = [jnp.full((MAX_ROWS, Q_TILE), -jnp.inf, F32)] * 2

    def item(it, e):
        return it * tiles_per_iter + e // npairs, e % npairs, s_bufs[e % 2]

    def score_chunk(tile, pair, s_ref, c, mparts):
        qrows = pl.ds(pl.multiple_of(tile * Q_TILE, Q_TILE), Q_TILE)
        krows = pl.ds(pl.multiple_of(c * KEY_CHUNK, KEY_CHUNK), KEY_CHUNK)
        out = []
        for e in range(2):
            j = 2 * pair + e
            q = q_ref[qrows, q_slots[j] * LANES:(q_slots[j] + 1) * LANES]
            k = k_ref[krows, k_slots[j] * LANES:(k_slots[j] + 1) * LANES]
            st = lax.dot_general(k, q, (((1,), (1,)), ((), ())), preferred_element_type=F32)
            s_ref[e, krows, :] = st
            part = jnp.max(st.reshape(KEY_CHUNK // MAX_ROWS, MAX_ROWS, Q_TILE), axis=0)
            out.append(jnp.maximum(mparts[e], part))
        return out

    def pv_chunk(pair, s_ref, c, m):
        krows = pl.ds(pl.multiple_of(c * KEY_CHUNK, KEY_CHUNK), KEY_CHUNK)
        for e in range(2):
            pt = jnp.exp2((s_ref[e, krows, :] - m[e]).astype(BF16))
            acc_ref[e] += jnp.dot(vt_ref[v_slots[2 * pair + e], c], pt, preferred_element_type=F32)

    def col_max(mparts):
        return [jnp.max(mp, axis=0, keepdims=True) for mp in mparts]

    def finish(tile, pair):
        halves = [acc_ref[e, :MLA_V, :] / acc_ref[e, MLA_V:MLA_V + 1, :] for e in range(2)]
        ot = jnp.concatenate(halves, axis=0)
        rows = pl.ds(pl.multiple_of(tile * Q_TILE, Q_TILE), Q_TILE)
        o_ref[rows, pair * LANES:(pair + 1) * LANES] = ot.T.astype(BF16)
        acc_ref[...] = jnp.zeros_like(acc_ref)

    def item_step(it, e, m_prev):
        tile, pair, s_cur = item(it, e)
        tile_p, pair_p, s_prev = item(it, e - 1) if e > 0 else item(it - 1, items_per_iter - 1)

        def chunk(c, mp):
            mp = score_chunk(tile, pair, s_cur, c, mp)
            pv_chunk(pair_p, s_prev, c, m_prev)
            return mp

        mparts = lax.fori_loop(0, nc, chunk, neg_inf, unroll=CHUNK_UNROLL)
        finish(tile_p, pair_p)
        return col_max(mparts)

    def trip(it, m, first):
        for e in range(first, items_per_iter):
            m = item_step(it, e, m)
        return m

    acc_ref[...] = jnp.zeros_like(acc_ref)
    tile0, pair0, s0 = item(0, 0)
    m = col_max(lax.fori_loop(0, nc, lambda c, mp: score_chunk(tile0, pair0, s0, c, mp), neg_inf,
                              unroll=CHUNK_UNROLL))
    m = trip(0, m, 1)
    m = lax.fori_loop(1, n_iter, lambda it, m: trip(it, m, 0), m)

    tile_l, pair_l, s_l = item(n_iter - 1, items_per_iter - 1)

    def last_chunk(c, carry):
        pv_chunk(pair_l, s_l, c, m)
        return carry

    lax.fori_loop(0, nc, last_chunk, 0, unroll=CHUNK_UNROLL)
    finish(tile_l, pair_l)


def _attn_call(q, k, v, batch, seq, q_slots, k_slots, v_slots, name):
    nh = len(q_slots)
    qw, kw, vw = ((max(s) + 1) * LANES for s in (q_slots, k_slots, v_slots))
    ow = nh // 2 * LANES
    steps = q.shape[1] // qw
    assert k.shape[1] == steps * kw and v.shape[1] == steps * vw
    return pl.pallas_call(
        functools.partial(_attn_kernel, q_slots=q_slots, k_slots=k_slots, v_slots=v_slots),
        grid=(batch, steps),
        in_specs=[
            pl.BlockSpec((seq, qw), lambda b, p: (b, p)),
            pl.BlockSpec((seq, kw), lambda b, p: (b, p)),
            pl.BlockSpec((seq, vw), lambda b, p: (b, p)),
        ],
        out_specs=pl.BlockSpec((seq, ow), lambda b, p: (b, p)),
        out_shape=jax.ShapeDtypeStruct((q.shape[0], steps * ow), BF16),
        scratch_shapes=[
            pltpu.VMEM((vw // LANES, seq // KEY_CHUNK, PV_ROWS, KEY_CHUNK), BF16),
            pltpu.VMEM((2, seq, Q_TILE), F32),
            pltpu.VMEM((2, seq, Q_TILE), F32),
            pltpu.VMEM((2, PV_ROWS, Q_TILE), F32),
        ],
        compiler_params=pltpu.CompilerParams(vmem_limit_bytes=VMEM_LIMIT_BYTES),
        name=name,
    )(q, k, v)


def _rope_tables(seq):
    rows = seq // GRID_W
    row = jnp.repeat(jnp.arange(rows, dtype=F32), GRID_W)
    col = jnp.tile(jnp.arange(GRID_W, dtype=F32), rows)

    def slot_tables(dim, lead, slot_w):
        half = dim // 2
        inv = ROPE_THETA ** (-jnp.arange(0, half, 2, dtype=F32) / half)
        ar, ac = row[:, None] * inv[None, :], col[:, None] * inv[None, :]
        z = jnp.zeros_like(ar)
        c = jnp.concatenate([jnp.cos(ar)] * 2 + [jnp.cos(ac)] * 2, axis=1)
        sa = jnp.concatenate([-jnp.sin(ar), z, -jnp.sin(ac), z], axis=1)
        sb = jnp.concatenate([z, jnp.sin(ar), z, jnp.sin(ac)], axis=1)
        tail = slot_w - lead - dim
        pad = lambda a, fill: jnp.concatenate(
            [jnp.full((seq, lead), fill, F32), a, jnp.full((seq, tail), fill, F32)], axis=1)
        return pad(c, 1.0), pad(sa, 0.0), pad(sb, 0.0)

    mla = slot_tables(MLA_ROPE, MLA_NOPE, LANES)
    gqa = [jnp.tile(a, (1, LANES // GQA_HD)) for a in slot_tables(GQA_HD, 0, GQA_HD)]
    return jnp.concatenate(list(mla) + gqa, axis=1)


def _layer_params(l, norm_ffn1, w_ffn1_gate, w_ffn1_up, w_ffn1_down, norm_mix, w_in, q_a_norm, w_q_b,
                  kv_a_norm, w_kv_b, gqa_q_norm, gqa_k_norm, w_out, norm_ffn2, w_ffn2_gate, w_ffn2_up,
                  w_ffn2_down):
    nch = D_FF // FF_CHUNK

    def gate_up(wg, wu):
        w = jnp.stack([wg.reshape(D_MODEL, nch, FF_CHUNK), wu.reshape(D_MODEL, nch, FF_CHUNK)], axis=2)
        return w.reshape(D_MODEL, 2 * D_FF).astype(BF16)

    row = lambda v: v.reshape(1, -1).astype(F32)
    wi = w_in[l]
    o = np.cumsum([0, Q_LORA, KV_LORA, MLA_ROPE, GQA_HEADS * GQA_HD, GQA_KV_HEADS * GQA_HD, GQA_KV_HEADS * GQA_HD])
    zc = lambda n: jnp.zeros((D_MODEL, n), F32)
    win = jnp.concatenate(
        [wi[:, o[0]:o[1]], wi[:, o[1]:o[2]], zc(MLA_NOPE), wi[:, o[2]:o[3]], zc(LANES - MLA_NOPE - MLA_ROPE),
         wi[:, o[3]:o[4]], wi[:, o[4]:o[5]], wi[:, o[5]:o[6]]], axis=1).astype(BF16)
    assert win.shape[1] == _ZW

    wqb = w_q_b[l].reshape(Q_LORA, MLA_HEADS, MLA_NOPE + MLA_ROPE)
    wqb = jnp.pad(wqb, ((0, 0), (0, 0), (0, LANES - MLA_NOPE - MLA_ROPE))).reshape(Q_LORA, MLA_HEADS * LANES)

    wkv = w_kv_b[l].reshape(KV_LORA, MLA_HEADS, MLA_NOPE + MLA_V)
    wk = jnp.pad(wkv[:, :, :MLA_NOPE], ((0, 0), (0, 0), (0, LANES - MLA_NOPE)))
    wv = jnp.pad(wkv[:, :, MLA_NOPE:], ((0, 0), (0, 0), (0, LANES - MLA_V)))
    wkvb = jnp.concatenate([wk, wv], axis=1).reshape(KV_LORA, 2 * MLA_HEADS * LANES)

    gmat = np.kron(np.eye(GQA_HEADS), np.full((GQA_HD, GQA_HD), 1.0 / GQA_HD))
    return dict(
        g1=row(norm_ffn1[l]), wgu1=gate_up(w_ffn1_gate[l], w_ffn1_up[l]), wd1=w_ffn1_down[l].astype(BF16),
        gmix=row(norm_mix[l]), win=win, gqa=row(q_a_norm[l]), wqb=wqb.astype(BF16), gkv=row(kv_a_norm[l]),
        wkvb=wkvb.astype(BF16),
        gq=row(jnp.tile(gqa_q_norm[l], GQA_HEADS)) * (GQA_HD ** -0.5 * LOG2E),
        gk=row(jnp.tile(gqa_k_norm[l], GQA_KV_HEADS)),
        gmat=jnp.asarray(gmat, BF16), wout=w_out[l].astype(BF16),
        g2=row(norm_ffn2[l]), wgu2=gate_up(w_ffn2_gate[l], w_ffn2_up[l]), wd2=w_ffn2_down[l].astype(BF16),
    )


def _trunk(x3, layers, tab, final_g):
    batch, seq, _ = x3.shape
    assert seq % ROW_TILE == 0 and seq % Q_TILE == 0 and seq % GRID_W == 0
    x = x3.reshape(batch * seq, D_MODEL)
    for li, p in enumerate(layers):
        x = _ffn_call(x, p["g1"], p["wgu1"], p["wd1"])
        qm, km, vm, qg, kg, vg = _prep_call(x, tab, seq, p)
        o_mla = _attn_call(qm, km, vm, batch, seq, (0, 1), (0, 1), (0, 1), "attn_mla")
        o_gqa = _attn_call(qg, kg, vg, batch, seq, (0, 0, 1, 1), (0, 1, 0, 1), (0, 0, 0, 0), "attn_gqa")
        x = _ffn_call(x, p["g2"], p["wgu2"], p["wd2"], attn=(o_mla, o_gqa, p["wout"]),
                      final_g=final_g if li == len(layers) - 1 else None)
    return x.reshape(batch, seq, D_MODEL)


def kernel(x_prompt, x_sample, norm_ffn1, w_ffn1_gate, w_ffn1_up, w_ffn1_down, norm_mix, w_in, q_a_norm, w_q_b, kv_a_norm, w_kv_b, gqa_q_norm, gqa_k_norm, w_out, norm_ffn2, w_ffn2_gate, w_ffn2_up, w_ffn2_down, final_norm):
    depth = norm_ffn1.shape[0]
    layers = [
        _layer_params(l, norm_ffn1, w_ffn1_gate, w_ffn1_up, w_ffn1_down, norm_mix, w_in, q_a_norm, w_q_b,
                      kv_a_norm, w_kv_b, gqa_q_norm, gqa_k_norm, w_out, norm_ffn2, w_ffn2_gate, w_ffn2_up,
                      w_ffn2_down)
        for l in range(depth)
    ]
    final_g = final_norm.reshape(1, -1).astype(F32)
    outs = []
    for x3 in (x_prompt, x_sample):
        tab = _rope_tables(x3.shape[1])
        outs.append(_trunk(x3, layers, tab, final_g))
    return tuple(outs)
```

```python
import functools

import jax
import jax.numpy as jnp
import numpy as np
from jax import lax
from jax.experimental import pallas as pl
from jax.experimental.pallas import tpu as pltpu

D_MODEL = 1024
GRID_W = 64
ROPE_THETA = 10000.0
EPS = 1e-6

MLA_HEADS = 8
MLA_NOPE = 64
MLA_ROPE = 32
MLA_V = 64
Q_LORA = 256
KV_LORA = 128
GQA_HEADS = 8
GQA_KV_HEADS = 2
GQA_HD = 64
D_FF = 2816

LANES = 128
LOG2E = 1.4426950408889634

ROW_TILE = 512
FF_CHUNK = 256
Q_TILE = 512
KEY_CHUNK = 512
PV_ROWS = 128
MAX_ROWS = 32
CHUNK_UNROLL = 4
VMEM_LIMIT_BYTES = 56 * 1024 * 1024

_ZW = Q_LORA + KV_LORA + LANES + GQA_HEADS * GQA_HD + 2 * LANES
_Z_CQ = (0, Q_LORA)
_Z_CKV = (Q_LORA, Q_LORA + KV_LORA)
_Z_KPE = (_Z_CKV[1], _Z_CKV[1] + LANES)
_Z_QG = (_Z_KPE[1], _Z_KPE[1] + GQA_HEADS * GQA_HD)
_Z_KG = (_Z_QG[1], _Z_QG[1] + LANES)
_Z_VG = (_Z_KG[1], _Z_KG[1] + LANES)

BF16 = jnp.bfloat16
F32 = jnp.float32


def _rms(x, g):
    ms = jnp.mean(x * x, axis=-1, keepdims=True)
    return x * lax.rsqrt(ms + EPS) * g


def _rope(x, c, sa, sb, d):
    return x * c + pltpu.roll(x, LANES - d, 1) * sa + pltpu.roll(x, d, 1) * sb


def _ffn_kernel(*refs, has_attn, final):
    it = iter(refs)
    x_ref = next(it)
    if has_attn:
        om_ref, og_ref, wo_ref = next(it), next(it), next(it)
    g_ref, wg_ref, wu_ref, wd_ref = next(it), next(it), next(it), next(it)
    if final:
        fg_ref = next(it)
    o_ref = next(it)
    a_ref = next(it)

    x = x_ref[...]
    if has_attn:
        half = om_ref.shape[1]
        x = x + jnp.dot(om_ref[...], wo_ref[:half, :], preferred_element_type=F32)
        x = x + jnp.dot(og_ref[...], wo_ref[half:, :], preferred_element_type=F32)
    hb = _rms(x, g_ref[...]).astype(BF16)
    for c in range(D_FF // FF_CHUNK):
        sl = slice(c * FF_CHUNK, (c + 1) * FF_CHUNK)
        g = jnp.dot(hb, wg_ref[:, sl], preferred_element_type=F32)
        u = jnp.dot(hb, wu_ref[:, sl], preferred_element_type=F32)
        a_ref[:, sl] = (g * jax.nn.sigmoid(g) * u).astype(BF16)
    y = x + 0.5 * jnp.dot(a_ref[...], wd_ref[...], preferred_element_type=F32)
    if final:
        y = _rms(y, fg_ref[...])
    o_ref[...] = y


def _const_spec(shape):
    return pl.BlockSpec(shape, lambda *_: (0,) * len(shape), pipeline_mode=pl.Buffered(1))


def _ffn_call(x, g, wg, wu, wd, attn=None, final_g=None):
    t = x.shape[0]
    row = lambda w: pl.BlockSpec((ROW_TILE, w), lambda i: (i, 0))
    args, specs = [x], [row(D_MODEL)]
    if attn is not None:
        o_mla, o_gqa, w_out = attn
        args += [o_mla, o_gqa, w_out]
        specs += [row(o_mla.shape[1]), row(o_gqa.shape[1]), _const_spec(w_out.shape)]
    args += [g, wg, wu, wd]
    specs += [_const_spec(a.shape) for a in (g, wg, wu, wd)]
    if final_g is not None:
        args.append(final_g)
        specs.append(_const_spec(final_g.shape))
    return pl.pallas_call(
        functools.partial(_ffn_kernel, has_attn=attn is not None, final=final_g is not None),
        grid=(t // ROW_TILE,),
        in_specs=specs,
        out_specs=row(D_MODEL),
        out_shape=jax.ShapeDtypeStruct((t, D_MODEL), F32),
        scratch_shapes=[pltpu.VMEM((ROW_TILE, D_FF), BF16)],
        compiler_params=pltpu.CompilerParams(vmem_limit_bytes=VMEM_LIMIT_BYTES),
        name="ffn_attn" if attn is not None else "ffn",
    )(*args)


def _prep_kernel(x_ref, tab_ref, gmix_ref, win_ref, gqa_ref, wqb_ref, gkv_ref, wkvb_ref, gq_ref, gk_ref,
                 gmat_ref, qm_ref, km_ref, vm_ref, qg_ref, kg_ref, vg_ref):
    hb = _rms(x_ref[...], gmix_ref[...]).astype(BF16)
    z = jnp.dot(hb, win_ref[...], preferred_element_type=F32)

    mc, msa, msb = (tab_ref[:, j * LANES:(j + 1) * LANES] for j in range(3))
    gc, gsa, gsb = (tab_ref[:, j * LANES:(j + 1) * LANES] for j in range(3, 6))
    lane = lax.broadcasted_iota(jnp.int32, (1, LANES), 1)
    lo = lane < GQA_HD
    one = (lane == MLA_V).astype(F32)

    cq = _rms(z[:, _Z_CQ[0]:_Z_CQ[1]], gqa_ref[...]).astype(BF16)
    qa = jnp.dot(cq, wqb_ref[...], preferred_element_type=F32)
    q_scale = (MLA_NOPE + MLA_ROPE) ** -0.5 * LOG2E
    for h in range(MLA_HEADS):
        sl = slice(h * LANES, (h + 1) * LANES)
        qm_ref[:, sl] = (_rope(qa[:, sl], mc, msa, msb, MLA_ROPE // 4) * q_scale).astype(BF16)

    ckv = _rms(z[:, _Z_CKV[0]:_Z_CKV[1]], gkv_ref[...]).astype(BF16)
    kv = jnp.dot(ckv, wkvb_ref[...], preferred_element_type=F32)
    kpe = _rope(z[:, _Z_KPE[0]:_Z_KPE[1]], mc, msa, msb, MLA_ROPE // 4)
    for h in range(MLA_HEADS):
        sl = slice(h * LANES, (h + 1) * LANES)
        km_ref[:, sl] = (kv[:, sl] + kpe).astype(BF16)
        vsl = slice((MLA_HEADS + h) * LANES, (MLA_HEADS + h + 1) * LANES)
        vm_ref[:, sl] = (kv[:, vsl] + one).astype(BF16)

    def head_mean_sq(v, gmat):
        sq = v * v
        hi = sq.astype(BF16)
        lo_part = (sq - hi.astype(F32)).astype(BF16)
        return (jnp.dot(hi, gmat, preferred_element_type=F32)
                + jnp.dot(lo_part, gmat, preferred_element_type=F32))

    qg = z[:, _Z_QG[0]:_Z_QG[1]]
    qg = qg * lax.rsqrt(head_mean_sq(qg, gmat_ref[...]) + EPS) * gq_ref[...]
    for j in range(GQA_HEADS // 2):
        sl = slice(j * LANES, (j + 1) * LANES)
        qg_ref[:, sl] = _rope(qg[:, sl], gc, gsa, gsb, GQA_HD // 4).astype(BF16)

    kg = z[:, _Z_KG[0]:_Z_KG[1]]
    kg = kg * lax.rsqrt(head_mean_sq(kg, gmat_ref[:LANES, :LANES]) + EPS) * gk_ref[...]
    kg = _rope(kg, gc, gsa, gsb, GQA_HD // 4)
    kg_sw = pltpu.roll(kg, GQA_HD, 1)
    zero = jnp.zeros_like(kg)
    kg_ref[:, 0 * LANES:1 * LANES] = jnp.where(lo, kg, zero).astype(BF16)
    kg_ref[:, 1 * LANES:2 * LANES] = jnp.where(lo, zero, kg_sw).astype(BF16)
    kg_ref[:, 2 * LANES:3 * LANES] = jnp.where(lo, kg_sw, zero).astype(BF16)
    kg_ref[:, 3 * LANES:4 * LANES] = jnp.where(lo, zero, kg).astype(BF16)

    vg = z[:, _Z_VG[0]:_Z_VG[1]]
    vg_sw = pltpu.roll(vg, GQA_HD, 1)
    vg_ref[:, 0 * LANES:1 * LANES] = jnp.where(lo, vg, one).astype(BF16)
    vg_ref[:, 1 * LANES:2 * LANES] = jnp.where(lo, vg_sw, one).astype(BF16)


def _prep_call(x, tab, seq, p):
    t = x.shape[0]
    tiles_per_seq = seq // ROW_TILE
    row = lambda w: pl.BlockSpec((ROW_TILE, w), lambda i: (i, 0))
    consts = [p["gmix"], p["win"], p["gqa"], p["wqb"], p["gkv"], p["wkvb"], p["gq"], p["gk"], p["gmat"]]
    widths = (MLA_HEADS * LANES,) * 3 + (GQA_HEADS // 2 * LANES, 2 * GQA_KV_HEADS * LANES, GQA_KV_HEADS * LANES)
    return pl.pallas_call(
        _prep_kernel,
        grid=(t // ROW_TILE,),
        in_specs=[row(D_MODEL), pl.BlockSpec((ROW_TILE, tab.shape[1]), lambda i: (i % tiles_per_seq, 0))]
        + [_const_spec(c.shape) for c in consts],
        out_specs=[row(w) for w in widths],
        out_shape=[jax.ShapeDtypeStruct((t, w), BF16) for w in widths],
        compiler_params=pltpu.CompilerParams(vmem_limit_bytes=VMEM_LIMIT_BYTES),
        name="prep",
    )(x, tab, *consts)


def _attn_kernel(q_ref, k_ref, v_ref, o_ref, vt_ref, sa_ref, sb_ref, acc_ref, *, q_slots, k_slots, v_slots):
    seq = k_ref.shape[0]
    nq, nc, npairs = seq // Q_TILE, seq // KEY_CHUNK, len(q_slots) // 2
    tiles_per_iter = 2 if npairs % 2 else 1
    items_per_iter = npairs * tiles_per_iter
    n_iter = nq // tiles_per_iter
    assert len(q_slots) % 2 == 0 and nq % tiles_per_iter == 0
    s_bufs = (sa_ref, sb_ref)
    for slot in sorted(set(v_slots)):
        for c in range(nc):
            vt = v_ref[c * KEY_CHUNK:(c + 1) * KEY_CHUNK, slot * LANES:(slot + 1) * LANES].T
            vt_ref[slot, c] = vt[:PV_ROWS, :]
    neg_inf = [jnp.full((MAX_ROWS, Q_TILE), -jnp.inf, F32)] * 2

    def item(it, e):
        return it * tiles_per_iter + e // npairs, e % npairs, s_bufs[e % 2]

    def score_chunk(tile, pair, s_ref, c, mparts):
        qrows = pl.ds(pl.multiple_of(tile * Q_TILE, Q_TILE), Q_TILE)
        krows = pl.ds(pl.multiple_of(c * KEY_CHUNK, KEY_CHUNK), KEY_CHUNK)
        out = []
        for e in range(2):
            j = 2 * pair + e
            q = q_ref[qrows, q_slots[j] * LANES:(q_slots[j] + 1) * LANES]
            k = k_ref[krows, k_slots[j] * LANES:(k_slots[j] + 1) * LANES]
            st = lax.dot_general(k, q, (((1,), (1,)), ((), ())), preferred_element_type=F32)
            s_ref[e, krows, :] = st
            part = jnp.max(st.reshape(KEY_CHUNK // MAX_ROWS, MAX_ROWS, Q_TILE), axis=0)
            out.append(jnp.maximum(mparts[e], part))
        return out

    def pv_chunk(pair, s_ref, c, m):
        krows = pl.ds(pl.multiple_of(c * KEY_CHUNK, KEY_CHUNK), KEY_CHUNK)
        for e in range(2):
            pt = jnp.exp2((s_ref[e, krows, :] - m[e]).astype(BF16))
            acc_ref[e] += jnp.dot(vt_ref[v_slots[2 * pair + e], c], pt, preferred_element_type=F32)

    def col_max(mparts):
        return [jnp.max(mp, axis=0, keepdims=True) for mp in mparts]

    def finish(tile, pair):
        halves = [acc_ref[e, :MLA_V, :] / acc_ref[e, MLA_V:MLA_V + 1, :] for e in range(2)]
        ot = jnp.concatenate(halves, axis=0)
        rows = pl.ds(pl.multiple_of(tile * Q_TILE, Q_TILE), Q_TILE)
        o_ref[rows, pair * LANES:(pair + 1) * LANES] = ot.T.astype(BF16)
        acc_ref[...] = jnp.zeros_like(acc_ref)

    def item_step(it, e, m_prev):
        tile, pair, s_cur = item(it, e)
        tile_p, pair_p, s_prev = item(it, e - 1) if e > 0 else item(it - 1, items_per_iter - 1)

        def chunk(c, mp):
            mp = score_chunk(tile, pair, s_cur, c, mp)
            pv_chunk(pair_p, s_prev, c, m_prev)
            return mp

        mparts = lax.fori_loop(0, nc, chunk, neg_inf, unroll=CHUNK_UNROLL)
        finish(tile_p, pair_p)
        return col_max(mparts)

    def trip(it, m, first):
        for e in range(first, items_per_iter):
            m = item_step(it, e, m)
        return m

    acc_ref[...] = jnp.zeros_like(acc_ref)
    tile0, pair0, s0 = item(0, 0)
    m = col_max(lax.fori_loop(0, nc, lambda c, mp: score_chunk(tile0, pair0, s0, c, mp), neg_inf,
                              unroll=CHUNK_UNROLL))
    m = trip(0, m, 1)
    m = lax.fori_loop(1, n_iter, lambda it, m: trip(it, m, 0), m)

    tile_l, pair_l, s_l = item(n_iter - 1, items_per_iter - 1)

    def last_chunk(c, carry):
        pv_chunk(pair_l, s_l, c, m)
        return carry

    lax.fori_loop(0, nc, last_chunk, 0, unroll=CHUNK_UNROLL)
    finish(tile_l, pair_l)


def _attn_call(q, k, v, batch, seq, q_slots, k_slots, v_slots, name):
    nh = len(q_slots)
    qw, kw, vw = ((max(s) + 1) * LANES for s in (q_slots, k_slots, v_slots))
    ow = nh // 2 * LANES
    steps = q.shape[1] // qw
    assert k.shape[1] == steps * kw and v.shape[1] == steps * vw
    return pl.pallas_call(
        functools.partial(_attn_kernel, q_slots=q_slots, k_slots=k_slots, v_slots=v_slots),
        grid=(batch, steps),
        in_specs=[
            pl.BlockSpec((seq, qw), lambda b, p: (b, p)),
            pl.BlockSpec((seq, kw), lambda b, p: (b, p)),
            pl.BlockSpec((seq, vw), lambda b, p: (b, p)),
        ],
        out_specs=pl.BlockSpec((seq, ow), lambda b, p: (b, p)),
        out_shape=jax.ShapeDtypeStruct((q.shape[0], steps * ow), BF16),
        scratch_shapes=[
            pltpu.VMEM((vw // LANES, seq // KEY_CHUNK, PV_ROWS, KEY_CHUNK), BF16),
            pltpu.VMEM((2, seq, Q_TILE), F32),
            pltpu.VMEM((2, seq, Q_TILE), F32),
            pltpu.VMEM((2, PV_ROWS, Q_TILE), F32),
        ],
        compiler_params=pltpu.CompilerParams(vmem_limit_bytes=VMEM_LIMIT_BYTES),
        name=name,
    )(q, k, v)


def _rope_tables(seq):
    rows = seq // GRID_W
    row = jnp.repeat(jnp.arange(rows, dtype=F32), GRID_W)
    col = jnp.tile(jnp.arange(GRID_W, dtype=F32), rows)

    def slot_tables(dim, lead, slot_w):
        half = dim // 2
        inv = ROPE_THETA ** (-jnp.arange(0, half, 2, dtype=F32) / half)
        ar, ac = row[:, None] * inv[None, :], col[:, None] * inv[None, :]
        z = jnp.zeros_like(ar)
        c = jnp.concatenate([jnp.cos(ar)] * 2 + [jnp.cos(ac)] * 2, axis=1)
        sa = jnp.concatenate([-jnp.sin(ar), z, -jnp.sin(ac), z], axis=1)
        sb = jnp.concatenate([z, jnp.sin(ar), z, jnp.sin(ac)], axis=1)
        tail = slot_w - lead - dim
        pad = lambda a, fill: jnp.concatenate(
            [jnp.full((seq, lead), fill, F32), a, jnp.full((seq, tail), fill, F32)], axis=1)
        return pad(c, 1.0), pad(sa, 0.0), pad(sb, 0.0)

    mla = slot_tables(MLA_ROPE, MLA_NOPE, LANES)
    gqa = [jnp.tile(a, (1, LANES // GQA_HD)) for a in slot_tables(GQA_HD, 0, GQA_HD)]
    return jnp.concatenate(list(mla) + gqa, axis=1)


def _layer_params(l, norm_ffn1, w_ffn1_gate, w_ffn1_up, w_ffn1_down, norm_mix, w_in, q_a_norm, w_q_b,
                  kv_a_norm, w_kv_b, gqa_q_norm, gqa_k_norm, w_out, norm_ffn2, w_ffn2_gate, w_ffn2_up,
                  w_ffn2_down):
    row = lambda v: v.reshape(1, -1).astype(F32)
    wi = w_in[l]
    o = np.cumsum([0, Q_LORA, KV_LORA, MLA_ROPE, GQA_HEADS * GQA_HD, GQA_KV_HEADS * GQA_HD, GQA_KV_HEADS * GQA_HD])
    zc = lambda n: jnp.zeros((D_MODEL, n), F32)
    win = jnp.concatenate(
        [wi[:, o[0]:o[1]], wi[:, o[1]:o[2]], zc(MLA_NOPE), wi[:, o[2]:o[3]], zc(LANES - MLA_NOPE - MLA_ROPE),
         wi[:, o[3]:o[4]], wi[:, o[4]:o[5]], wi[:, o[5]:o[6]]], axis=1).astype(BF16)
    assert win.shape[1] == _ZW

    wqb = w_q_b[l].reshape(Q_LORA, MLA_HEADS, MLA_NOPE + MLA_ROPE)
    wqb = jnp.pad(wqb, ((0, 0), (0, 0), (0, LANES - MLA_NOPE - MLA_ROPE))).reshape(Q_LORA, MLA_HEADS * LANES)

    wkv = w_kv_b[l].reshape(KV_LORA, MLA_HEADS, MLA_NOPE + MLA_V)
    wk = jnp.pad(wkv[:, :, :MLA_NOPE], ((0, 0), (0, 0), (0, LANES - MLA_NOPE)))
    wv = jnp.pad(wkv[:, :, MLA_NOPE:], ((0, 0), (0, 0), (0, LANES - MLA_V)))
    wkvb = jnp.concatenate([wk, wv], axis=1).reshape(KV_LORA, 2 * MLA_HEADS * LANES)

    gmat = np.kron(np.eye(GQA_HEADS), np.full((GQA_HD, GQA_HD), 1.0 / GQA_HD))
    return dict(
        g1=row(norm_ffn1[l]), wg1=w_ffn1_gate[l].astype(BF16), wu1=w_ffn1_up[l].astype(BF16),
        wd1=w_ffn1_down[l].astype(BF16),
        gmix=row(norm_mix[l]), win=win, gqa=row(q_a_norm[l]), wqb=wqb.astype(BF16), gkv=row(kv_a_norm[l]),
        wkvb=wkvb.astype(BF16),
        gq=row(jnp.tile(gqa_q_norm[l], GQA_HEADS)) * (GQA_HD ** -0.5 * LOG2E),
        gk=row(jnp.tile(gqa_k_norm[l], GQA_KV_HEADS)),
        gmat=jnp.asarray(gmat, BF16), wout=w_out[l].astype(BF16),
        g2=row(norm_ffn2[l]), wg2=w_ffn2_gate[l].astype(BF16), wu2=w_ffn2_up[l].astype(BF16),
        wd2=w_ffn2_down[l].astype(BF16),
    )


def _trunk(x3, layers, tab, final_g):
    batch, seq, _ = x3.shape
    assert seq % ROW_TILE == 0 and seq % Q_TILE == 0 and seq % GRID_W == 0
    x = x3.reshape(batch * seq, D_MODEL)
    for li, p in enumerate(layers):
        x = _ffn_call(x, p["g1"], p["wg1"], p["wu1"], p["wd1"])
        qm, km, vm, qg, kg, vg = _prep_call(x, tab, seq, p)
        o_mla = _attn_call(qm, km, vm, batch, seq, (0, 1), (0, 1), (0, 1), "attn_mla")
        o_gqa = _attn_call(qg, kg, vg, batch, seq, (0, 0, 1, 1), (0, 1, 0, 1), (0, 0, 0, 0), "attn_gqa")
        x = _ffn_call(x, p["g2"], p["wg2"], p["wu2"], p["wd2"], attn=(o_mla, o_gqa, p["wout"]),
                      final_g=final_g if li == len(layers) - 1 else None)
    return x.reshape(batch, seq, D_MODEL)


def kernel(x_prompt, x_sample, norm_ffn1, w_ffn1_gate, w_ffn1_up, w_ffn1_down, norm_mix, w_in, q_a_norm, w_q_b, kv_a_norm, w_kv_b, gqa_q_norm, gqa_k_norm, w_out, norm_ffn2, w_ffn2_gate, w_ffn2_up, w_ffn2_down, final_norm):
    depth = norm_ffn1.shape[0]
    layers = [
        _layer_params(l, norm_ffn1, w_ffn1_gate, w_ffn1_up, w_ffn1_down, norm_mix, w_in, q_a_norm, w_q_b,
                      kv_a_norm, w_kv_b, gqa_q_norm, gqa_k_norm, w_out, norm_ffn2, w_ffn2_gate, w_ffn2_up,
                      w_ffn2_down)
        for l in range(depth)
    ]
    final_g = final_norm.reshape(1, -1).astype(F32)
    outs = []
    for x3 in (x_prompt, x_sample):
        tab = _rope_tables(x3.shape[1])
        outs.append(_trunk(x3, layers, tab, final_g))
    return tuple(outs)
```

```python
import functools

import jax
import jax.numpy as jnp
import numpy as np
from jax import lax
from jax.experimental import pallas as pl
from jax.experimental.pallas import tpu as pltpu

D_MODEL = 1024
GRID_W = 64
ROPE_THETA = 10000.0
EPS = 1e-6

MLA_HEADS = 8
MLA_NOPE = 64
MLA_ROPE = 32
MLA_V = 64
Q_LORA = 256
KV_LORA = 128
GQA_HEADS = 8
GQA_KV_HEADS = 2
GQA_HD = 64
D_FF = 2816

LANES = 128
LOG2E = 1.4426950408889634

ROW_TILE = 1024
FF_CHUNK = 256
Q_TILE = 512
KEY_CHUNK = 512
PV_ROWS = 128
MAX_ROWS = 8
CHUNK_UNROLL = 4
VMEM_LIMIT_BYTES = 56 * 1024 * 1024

_ZW = Q_LORA + KV_LORA + LANES + GQA_HEADS * GQA_HD + 2 * LANES
_Z_CQ = (0, Q_LORA)
_Z_CKV = (Q_LORA, Q_LORA + KV_LORA)
_Z_KPE = (_Z_CKV[1], _Z_CKV[1] + LANES)
_Z_QG = (_Z_KPE[1], _Z_KPE[1] + GQA_HEADS * GQA_HD)
_Z_KG = (_Z_QG[1], _Z_QG[1] + LANES)
_Z_VG = (_Z_KG[1], _Z_KG[1] + LANES)

BF16 = jnp.bfloat16
F32 = jnp.float32


def _rms(x, g):
    ms = jnp.mean(x * x, axis=-1, keepdims=True)
    return x * lax.rsqrt(ms + EPS) * g


def _rope(x, c, sa, sb, d):
    return x * c + pltpu.roll(x, LANES - d, 1) * sa + pltpu.roll(x, d, 1) * sb


def _ffn_kernel(*refs, has_attn, final):
    it = iter(refs)
    x_ref = next(it)
    if has_attn:
        om_ref, og_ref, wo_ref = next(it), next(it), next(it)
    g_ref, wg_ref, wu_ref, wd_ref = next(it), next(it), next(it), next(it)
    if final:
        fg_ref = next(it)
    o_ref = next(it)
    a_ref = next(it)

    x = x_ref[...]
    if has_attn:
        half = om_ref.shape[1]
        x = x + jnp.dot(om_ref[...], wo_ref[:half, :], preferred_element_type=F32)
        x = x + jnp.dot(og_ref[...], wo_ref[half:, :], preferred_element_type=F32)
    hb = _rms(x, g_ref[...]).astype(BF16)
    for c in range(D_FF // FF_CHUNK):
        sl = slice(c * FF_CHUNK, (c + 1) * FF_CHUNK)
        g = jnp.dot(hb, wg_ref[:, sl], preferred_element_type=F32)
        u = jnp.dot(hb, wu_ref[:, sl], preferred_element_type=F32)
        a_ref[:, sl] = (g * jax.nn.sigmoid(g) * u).astype(BF16)
    y = x + 0.5 * jnp.dot(a_ref[...], wd_ref[...], preferred_element_type=F32)
    if final:
        y = _rms(y, fg_ref[...])
    o_ref[...] = y


def _const_spec(shape):
    return pl.BlockSpec(shape, lambda *_: (0,) * len(shape), pipeline_mode=pl.Buffered(1))


def _ffn_call(x, g, wg, wu, wd, attn=None, final_g=None):
    t = x.shape[0]
    row = lambda w: pl.BlockSpec((ROW_TILE, w), lambda i: (i, 0))
    args, specs = [x], [row(D_MODEL)]
    if attn is not None:
        o_mla, o_gqa, w_out = attn
        args += [o_mla, o_gqa, w_out]
        specs += [row(o_mla.shape[1]), row(o_gqa.shape[1]), _const_spec(w_out.shape)]
    args += [g, wg, wu, wd]
    specs += [_const_spec(a.shape) for a in (g, wg, wu, wd)]
    if final_g is not None:
        args.append(final_g)
        specs.append(_const_spec(final_g.shape))
    return pl.pallas_call(
        functools.partial(_ffn_kernel, has_attn=attn is not None, final=final_g is not None),
        grid=(t // ROW_TILE,),
        in_specs=specs,
        out_specs=row(D_MODEL),
        out_shape=jax.ShapeDtypeStruct((t, D_MODEL), F32),
        scratch_shapes=[pltpu.VMEM((ROW_TILE, D_FF), BF16)],
        compiler_params=pltpu.CompilerParams(vmem_limit_bytes=VMEM_LIMIT_BYTES),
        name="ffn_attn" if attn is not None else "ffn",
    )(*args)


def _prep_kernel(x_ref, tab_ref, gmix_ref, win_ref, gqa_ref, wqb_ref, gkv_ref, wkvb_ref, gq_ref, gk_ref,
                 gmat_ref, qm_ref, km_ref, vm_ref, qg_ref, kg_ref, vg_ref):
    hb = _rms(x_ref[...], gmix_ref[...]).astype(BF16)
    z = jnp.dot(hb, win_ref[...], preferred_element_type=F32)

    mc, msa, msb = (tab_ref[:, j * LANES:(j + 1) * LANES] for j in range(3))
    gc, gsa, gsb = (tab_ref[:, j * LANES:(j + 1) * LANES] for j in range(3, 6))
    lane = lax.broadcasted_iota(jnp.int32, (1, LANES), 1)
    lo = lane < GQA_HD
    one = (lane == MLA_V).astype(F32)

    cq = _rms(z[:, _Z_CQ[0]:_Z_CQ[1]], gqa_ref[...]).astype(BF16)
    qa = jnp.dot(cq, wqb_ref[...], preferred_element_type=F32)
    q_scale = (MLA_NOPE + MLA_ROPE) ** -0.5 * LOG2E
    for h in range(MLA_HEADS):
        sl = slice(h * LANES, (h + 1) * LANES)
        qm_ref[:, sl] = (_rope(qa[:, sl], mc, msa, msb, MLA_ROPE // 4) * q_scale).astype(BF16)

    ckv = _rms(z[:, _Z_CKV[0]:_Z_CKV[1]], gkv_ref[...]).astype(BF16)
    kv = jnp.dot(ckv, wkvb_ref[...], preferred_element_type=F32)
    kpe = _rope(z[:, _Z_KPE[0]:_Z_KPE[1]], mc, msa, msb, MLA_ROPE // 4)
    for h in range(MLA_HEADS):
        sl = slice(h * LANES, (h + 1) * LANES)
        km_ref[:, sl] = (kv[:, sl] + kpe).astype(BF16)
        vsl = slice((MLA_HEADS + h) * LANES, (MLA_HEADS + h + 1) * LANES)
        vm_ref[:, sl] = (kv[:, vsl] + one).astype(BF16)

    def head_mean_sq(v, gmat):
        sq = v * v
        hi = sq.astype(BF16)
        lo_part = (sq - hi.astype(F32)).astype(BF16)
        return (jnp.dot(hi, gmat, preferred_element_type=F32)
                + jnp.dot(lo_part, gmat, preferred_element_type=F32))

    qg = z[:, _Z_QG[0]:_Z_QG[1]]
    qg = qg * lax.rsqrt(head_mean_sq(qg, gmat_ref[...]) + EPS) * gq_ref[...]
    for j in range(GQA_HEADS // 2):
        sl = slice(j * LANES, (j + 1) * LANES)
        qg_ref[:, sl] = _rope(qg[:, sl], gc, gsa, gsb, GQA_HD // 4).astype(BF16)

    kg = z[:, _Z_KG[0]:_Z_KG[1]]
    kg = kg * lax.rsqrt(head_mean_sq(kg, gmat_ref[:LANES, :LANES]) + EPS) * gk_ref[...]
    kg = _rope(kg, gc, gsa, gsb, GQA_HD // 4)
    kg_sw = pltpu.roll(kg, GQA_HD, 1)
    zero = jnp.zeros_like(kg)
    kg_ref[:, 0 * LANES:1 * LANES] = jnp.where(lo, kg, zero).astype(BF16)
    kg_ref[:, 1 * LANES:2 * LANES] = jnp.where(lo, zero, kg_sw).astype(BF16)
    kg_ref[:, 2 * LANES:3 * LANES] = jnp.where(lo, kg_sw, zero).astype(BF16)
    kg_ref[:, 3 * LANES:4 * LANES] = jnp.where(lo, zero, kg).astype(BF16)

    vg = z[:, _Z_VG[0]:_Z_VG[1]]
    vg_sw = pltpu.roll(vg, GQA_HD, 1)
    vg_ref[:, 0 * LANES:1 * LANES] = jnp.where(lo, vg, one).astype(BF16)
    vg_ref[:, 1 * LANES:2 * LANES] = jnp.where(lo, vg_sw, one).astype(BF16)


def _prep_call(x, tab, seq, p):
    t = x.shape[0]
    tiles_per_seq = seq // ROW_TILE
    row = lambda w: pl.BlockSpec((ROW_TILE, w), lambda i: (i, 0))
    consts = [p["gmix"], p["win"], p["gqa"], p["wqb"], p["gkv"], p["wkvb"], p["gq"], p["gk"], p["gmat"]]
    widths = (MLA_HEADS * LANES,) * 3 + (GQA_HEADS // 2 * LANES, 2 * GQA_KV_HEADS * LANES, GQA_KV_HEADS * LANES)
    return pl.pallas_call(
        _prep_kernel,
        grid=(t // ROW_TILE,),
        in_specs=[row(D_MODEL), pl.BlockSpec((ROW_TILE, tab.shape[1]), lambda i: (i % tiles_per_seq, 0))]
        + [_const_spec(c.shape) for c in consts],
        out_specs=[row(w) for w in widths],
        out_shape=[jax.ShapeDtypeStruct((t, w), BF16) for w in widths],
        compiler_params=pltpu.CompilerParams(vmem_limit_bytes=VMEM_LIMIT_BYTES),
        name="prep",
    )(x, tab, *consts)


def _attn_kernel(q_ref, k_ref, v_ref, o_ref, vt_ref, sa_ref, sb_ref, acc_ref, *, q_slots, k_slots, v_slots):
    seq = k_ref.shape[0]
    nq, nc, npairs = seq // Q_TILE, seq // KEY_CHUNK, len(q_slots) // 2
    tiles_per_iter = 2 if npairs % 2 else 1
    items_per_iter = npairs * tiles_per_iter
    n_iter = nq // tiles_per_iter
    assert len(q_slots) % 2 == 0 and nq % tiles_per_iter == 0
    s_bufs = (sa_ref, sb_ref)
    for slot in sorted(set(v_slots)):
        for c in range(nc):
            vt = v_ref[c * KEY_CHUNK:(c + 1) * KEY_CHUNK, slot * LANES:(slot + 1) * LANES].T
            vt_ref[slot, c] = vt[:PV_ROWS, :]
    neg_inf = [jnp.full((MAX_ROWS, Q_TILE), -jnp.inf, F32)] * 2

    def item(it, e):
        return it * tiles_per_iter + e // npairs, e % npairs, s_bufs[e % 2]

    def score_chunk(tile, pair, s_ref, c, mparts):
        qrows = pl.ds(pl.multiple_of(tile * Q_TILE, Q_TILE), Q_TILE)
        krows = pl.ds(pl.multiple_of(c * KEY_CHUNK, KEY_CHUNK), KEY_CHUNK)
        out = []
        for e in range(2):
            j = 2 * pair + e
            q = q_ref[qrows, q_slots[j] * LANES:(q_slots[j] + 1) * LANES]
            k = k_ref[krows, k_slots[j] * LANES:(k_slots[j] + 1) * LANES]
            st = lax.dot_general(k, q, (((1,), (1,)), ((), ())), preferred_element_type=F32)
            s_ref[e, krows, :] = st
            part = jnp.max(st.reshape(KEY_CHUNK // MAX_ROWS, MAX_ROWS, Q_TILE), axis=0)
            out.append(jnp.maximum(mparts[e], part))
        return out

    def pv_chunk(pair, s_ref, c, m):
        krows = pl.ds(pl.multiple_of(c * KEY_CHUNK, KEY_CHUNK), KEY_CHUNK)
        for e in range(2):
            pt = jnp.exp2((s_ref[e, krows, :] - m[e]).astype(BF16))
            acc_ref[e] += jnp.dot(vt_ref[v_slots[2 * pair + e], c], pt, preferred_element_type=F32)

    def col_max(mparts):
        return [jnp.max(mp, axis=0, keepdims=True) for mp in mparts]

    def finish(tile, pair):
        halves = [acc_ref[e, :MLA_V, :] / acc_ref[e, MLA_V:MLA_V + 1, :] for e in range(2)]
        ot = jnp.concatenate(halves, axis=0)
        rows = pl.ds(pl.multiple_of(tile * Q_TILE, Q_TILE), Q_TILE)
        o_ref[rows, pair * LANES:(pair + 1) * LANES] = ot.T.astype(BF16)
        acc_ref[...] = jnp.zeros_like(acc_ref)

    def item_step(it, e, m_prev):
        tile, pair, s_cur = item(it, e)
        tile_p, pair_p, s_prev = item(it, e - 1) if e > 0 else item(it - 1, items_per_iter - 1)

        def chunk(c, mp):
            mp = score_chunk(tile, pair, s_cur, c, mp)
            pv_chunk(pair_p, s_prev, c, m_prev)
            return mp

        mparts = lax.fori_loop(0, nc, chunk, neg_inf, unroll=CHUNK_UNROLL)
        finish(tile_p, pair_p)
        return col_max(mparts)

    def trip(it, m, first):
        for e in range(first, items_per_iter):
            m = item_step(it, e, m)
        return m

    acc_ref[...] = jnp.zeros_like(acc_ref)
    tile0, pair0, s0 = item(0, 0)
    m = col_max(lax.fori_loop(0, nc, lambda c, mp: score_chunk(tile0, pair0, s0, c, mp), neg_inf,
                              unroll=CHUNK_UNROLL))
    m = trip(0, m, 1)
    m = lax.fori_loop(1, n_iter, lambda it, m: trip(it, m, 0), m)

    tile_l, pair_l, s_l = item(n_iter - 1, items_per_iter - 1)

    def last_chunk(c, carry):
        pv_chunk(pair_l, s_l, c, m)
        return carry

    lax.fori_loop(0, nc, last_chunk, 0, unroll=CHUNK_UNROLL)
    finish(tile_l, pair_l)


def _attn_call(q, k, v, batch, seq, q_slots, k_slots, v_slots, name):
    nh = len(q_slots)
    qw, kw, vw = ((max(s) + 1) * LANES for s in (q_slots, k_slots, v_slots))
    ow = nh // 2 * LANES
    steps = q.shape[1] // qw
    assert k.shape[1] == steps * kw and v.shape[1] == steps * vw
    return pl.pallas_call(
        functools.partial(_attn_kernel, q_slots=q_slots, k_slots=k_slots, v_slots=v_slots),
        grid=(batch, steps),
        in_specs=[
            pl.BlockSpec((seq, qw), lambda b, p: (b, p)),
            pl.BlockSpec((seq, kw), lambda b, p: (b, p)),
            pl.BlockSpec((seq, vw), lambda b, p: (b, p)),
        ],
        out_specs=pl.BlockSpec((seq, ow), lambda b, p: (b, p)),
        out_shape=jax.ShapeDtypeStruct((q.shape[0], steps * ow), BF16),
        scratch_shapes=[
            pltpu.VMEM((vw // LANES, seq // KEY_CHUNK, PV_ROWS, KEY_CHUNK), BF16),
            pltpu.VMEM((2, seq, Q_TILE), F32),
            pltpu.VMEM((2, seq, Q_TILE), F32),
            pltpu.VMEM((2, PV_ROWS, Q_TILE), F32),
        ],
        compiler_params=pltpu.CompilerParams(vmem_limit_bytes=VMEM_LIMIT_BYTES),
        name=name,
    )(q, k, v)


def _rope_tables(seq):
    rows = seq // GRID_W
    row = jnp.repeat(jnp.arange(rows, dtype=F32), GRID_W)
    col = jnp.tile(jnp.arange(GRID_W, dtype=F32), rows)

    def slot_tables(dim, lead, slot_w):
        half = dim // 2
        inv = ROPE_THETA ** (-jnp.arange(0, half, 2, dtype=F32) / half)
        ar, ac = row[:, None] * inv[None, :], col[:, None] * inv[None, :]
        z = jnp.zeros_like(ar)
        c = jnp.concatenate([jnp.cos(ar)] * 2 + [jnp.cos(ac)] * 2, axis=1)
        sa = jnp.concatenate([-jnp.sin(ar), z, -jnp.sin(ac), z], axis=1)
        sb = jnp.concatenate([z, jnp.sin(ar), z, jnp.sin(ac)], axis=1)
        tail = slot_w - lead - dim
        pad = lambda a, fill: jnp.concatenate(
            [jnp.full((seq, lead), fill, F32), a, jnp.full((seq, tail), fill, F32)], axis=1)
        return pad(c, 1.0), pad(sa, 0.0), pad(sb, 0.0)

    mla = slot_tables(MLA_ROPE, MLA_NOPE, LANES)
    gqa = [jnp.tile(a, (1, LANES // GQA_HD)) for a in slot_tables(GQA_HD, 0, GQA_HD)]
    return jnp.concatenate(list(mla) + gqa, axis=1)


def _layer_params(l, norm_ffn1, w_ffn1_gate, w_ffn1_up, w_ffn1_down, norm_mix, w_in, q_a_norm, w_q_b,
                  kv_a_norm, w_kv_b, gqa_q_norm, gqa_k_norm, w_out, norm_ffn2, w_ffn2_gate, w_ffn2_up,
                  w_ffn2_down):
    row = lambda v: v.reshape(1, -1).astype(F32)
    wi = w_in[l]
    o = np.cumsum([0, Q_LORA, KV_LORA, MLA_ROPE, GQA_HEADS * GQA_HD, GQA_KV_HEADS * GQA_HD, GQA_KV_HEADS * GQA_HD])
    zc = lambda n: jnp.zeros((D_MODEL, n), F32)
    win = jnp.concatenate(
        [wi[:, o[0]:o[1]], wi[:, o[1]:o[2]], zc(MLA_NOPE), wi[:, o[2]:o[3]], zc(LANES - MLA_NOPE - MLA_ROPE),
         wi[:, o[3]:o[4]], wi[:, o[4]:o[5]], wi[:, o[5]:o[6]]], axis=1).astype(BF16)
    assert win.shape[1] == _ZW

    wqb = w_q_b[l].reshape(Q_LORA, MLA_HEADS, MLA_NOPE + MLA_ROPE)
    wqb = jnp.pad(wqb, ((0, 0), (0, 0), (0, LANES - MLA_NOPE - MLA_ROPE))).reshape(Q_LORA, MLA_HEADS * LANES)

    wkv = w_kv_b[l].reshape(KV_LORA, MLA_HEADS, MLA_NOPE + MLA_V)
    wk = jnp.pad(wkv[:, :, :MLA_NOPE], ((0, 0), (0, 0), (0, LANES - MLA_NOPE)))
    wv = jnp.pad(wkv[:, :, MLA_NOPE:], ((0, 0), (0, 0), (0, LANES - MLA_V)))
    wkvb = jnp.concatenate([wk, wv], axis=1).reshape(KV_LORA, 2 * MLA_HEADS * LANES)

    gmat = np.kron(np.eye(GQA_HEADS), np.full((GQA_HD, GQA_HD), 1.0 / GQA_HD))
    return dict(
        g1=row(norm_ffn1[l]), wg1=w_ffn1_gate[l].astype(BF16), wu1=w_ffn1_up[l].astype(BF16),
        wd1=w_ffn1_down[l].astype(BF16),
        gmix=row(norm_mix[l]), win=win, gqa=row(q_a_norm[l]), wqb=wqb.astype(BF16), gkv=row(kv_a_norm[l]),
        wkvb=wkvb.astype(BF16),
        gq=row(jnp.tile(gqa_q_norm[l], GQA_HEADS)) * (GQA_HD ** -0.5 * LOG2E),
        gk=row(jnp.tile(gqa_k_norm[l], GQA_KV_HEADS)),
        gmat=jnp.asarray(gmat, BF16), wout=w_out[l].astype(BF16),
        g2=row(norm_ffn2[l]), wg2=w_ffn2_gate[l].astype(BF16), wu2=w_ffn2_up[l].astype(BF16),
        wd2=w_ffn2_down[l].astype(BF16),
    )


def _trunk(x3, layers, tab, final_g):
    batch, seq, _ = x3.shape
    assert seq % ROW_TILE == 0 and seq % Q_TILE == 0 and seq % GRID_W == 0
    x = x3.reshape(batch * seq, D_MODEL)
    for li, p in enumerate(layers):
        x = _ffn_call(x, p["g1"], p["wg1"], p["wu1"], p["wd1"])
        qm, km, vm, qg, kg, vg = _prep_call(x, tab, seq, p)
        o_mla = _attn_call(qm, km, vm, batch, seq, (0, 1), (0, 1), (0, 1), "attn_mla")
        o_gqa = _attn_call(qg, kg, vg, batch, seq, (0, 0, 1, 1), (0, 1, 0, 1), (0, 0, 0, 0), "attn_gqa")
        x = _ffn_call(x, p["g2"], p["wg2"], p["wu2"], p["wd2"], attn=(o_mla, o_gqa, p["wout"]),
                      final_g=final_g if li == len(layers) - 1 else None)
    return x.reshape(batch, seq, D_MODEL)


def kernel(x_prompt, x_sample, norm_ffn1, w_ffn1_gate, w_ffn1_up, w_ffn1_down, norm_mix, w_in, q_a_norm, w_q_b, kv_a_norm, w_kv_b, gqa_q_norm, gqa_k_norm, w_out, norm_ffn2, w_ffn2_gate, w_ffn2_up, w_ffn2_down, final_norm):
    depth = norm_ffn1.shape[0]
    layers = [
        _layer_params(l, norm_ffn1, w_ffn1_gate, w_ffn1_up, w_ffn1_down, norm_mix, w_in, q_a_norm, w_q_b,
                      kv_a_norm, w_kv_b, gqa_q_norm, gqa_k_norm, w_out, norm_ffn2, w_ffn2_gate, w_ffn2_up,
                      w_ffn2_down)
        for l in range(depth)
    ]
    final_g = final_norm.reshape(1, -1).astype(F32)
    outs = []
    for x3 in (x_prompt, x_sample):
        tab = _rope_tables(x3.shape[1])
        outs.append(_trunk(x3, layers, tab, final_g))
    return tuple(outs)
```

```python
import functools

import jax
import jax.numpy as jnp
import numpy as np
from jax import lax
from jax.experimental import pallas as pl
from jax.experimental.pallas import tpu as pltpu

D_MODEL = 1024
GRID_W = 64
ROPE_THETA = 10000.0
EPS = 1e-6

MLA_HEADS = 8
MLA_NOPE = 64
MLA_ROPE = 32
MLA_V = 64
Q_LORA = 256
KV_LORA = 128
GQA_HEADS = 8
GQA_KV_HEADS = 2
GQA_HD = 64
D_FF = 2816

LANES = 128
LOG2E = 1.4426950408889634

ROW_TILE = 1024
FF_CHUNK = 256
Q_TILE = 512
KEY_CHUNK = 512
PV_ROWS = 128
MAX_ROWS = 32
CHUNK_UNROLL = 4
VMEM_LIMIT_BYTES = 56 * 1024 * 1024

_ZW = Q_LORA + KV_LORA + LANES + GQA_HEADS * GQA_HD + 2 * LANES
_Z_CQ = (0, Q_LORA)
_Z_CKV = (Q_LORA, Q_LORA + KV_LORA)
_Z_KPE = (_Z_CKV[1], _Z_CKV[1] + LANES)
_Z_QG = (_Z_KPE[1], _Z_KPE[1] + GQA_HEADS * GQA_HD)
_Z_KG = (_Z_QG[1], _Z_QG[1] + LANES)
_Z_VG = (_Z_KG[1], _Z_KG[1] + LANES)

BF16 = jnp.bfloat16
F32 = jnp.float32


def _rms(x, g):
    ms = jnp.mean(x * x, axis=-1, keepdims=True)
    return x * lax.rsqrt(ms + EPS) * g


def _rope(x, c, sa, sb, d):
    return x * c + pltpu.roll(x, LANES - d, 1) * sa + pltpu.roll(x, d, 1) * sb


def _ffn_kernel(*refs, has_attn, final):
    it = iter(refs)
    x_ref = next(it)
    if has_attn:
        om_ref, og_ref, wo_ref = next(it), next(it), next(it)
    g_ref, wg_ref, wu_ref, wd_ref = next(it), next(it), next(it), next(it)
    if final:
        fg_ref = next(it)
    o_ref = next(it)
    a_ref = next(it)

    x = x_ref[...]
    if has_attn:
        half = om_ref.shape[1]
        x = x + jnp.dot(om_ref[...], wo_ref[:half, :], preferred_element_type=F32)
        x = x + jnp.dot(og_ref[...], wo_ref[half:, :], preferred_element_type=F32)
    hb = _rms(x, g_ref[...]).astype(BF16)
    for c in range(D_FF // FF_CHUNK):
        sl = slice(c * FF_CHUNK, (c + 1) * FF_CHUNK)
        g = jnp.dot(hb, wg_ref[:, sl], preferred_element_type=F32)
        u = jnp.dot(hb, wu_ref[:, sl], preferred_element_type=F32)
        a_ref[:, sl] = (g * jax.nn.sigmoid(g) * u).astype(BF16)
    y = x + 0.5 * jnp.dot(a_ref[...], wd_ref[...], preferred_element_type=F32)
    if final:
        y = _rms(y, fg_ref[...])
    o_ref[...] = y


def _const_spec(shape):
    return pl.BlockSpec(shape, lambda *_: (0,) * len(shape), pipeline_mode=pl.Buffered(1))


def _ffn_call(x, g, wg, wu, wd, attn=None, final_g=None):
    t = x.shape[0]
    row = lambda w: pl.BlockSpec((ROW_TILE, w), lambda i: (i, 0))
    args, specs = [x], [row(D_MODEL)]
    if attn is not None:
        o_mla, o_gqa, w_out = attn
        args += [o_mla, o_gqa, w_out]
        specs += [row(o_mla.shape[1]), row(o_gqa.shape[1]), _const_spec(w_out.shape)]
    args += [g, wg, wu, wd]
    specs += [_const_spec(a.shape) for a in (g, wg, wu, wd)]
    if final_g is not None:
        args.append(final_g)
        specs.append(_const_spec(final_g.shape))
    return pl.pallas_call(
        functools.partial(_ffn_kernel, has_attn=attn is not None, final=final_g is not None),
        grid=(t // ROW_TILE,),
        in_specs=specs,
        out_specs=row(D_MODEL),
        out_shape=jax.ShapeDtypeStruct((t, D_MODEL), F32),
        scratch_shapes=[pltpu.VMEM((ROW_TILE, D_FF), BF16)],
        compiler_params=pltpu.CompilerParams(vmem_limit_bytes=VMEM_LIMIT_BYTES),
        name="ffn_attn" if attn is not None else "ffn",
    )(*args)


def _prep_kernel(x_ref, tab_ref, gmix_ref, win_ref, gqa_ref, wqb_ref, gkv_ref, wkvb_ref, gq_ref, gk_ref,
                 gmat_ref, qm_ref, km_ref, vm_ref, qg_ref, kg_ref, vg_ref):
    hb = _rms(x_ref[...], gmix_ref[...]).astype(BF16)
    z = jnp.dot(hb, win_ref[...], preferred_element_type=F32)

    mc, msa, msb = (tab_ref[:, j * LANES:(j + 1) * LANES] for j in range(3))
    gc, gsa, gsb = (tab_ref[:, j * LANES:(j + 1) * LANES] for j in range(3, 6))
    lane = lax.broadcasted_iota(jnp.int32, (1, LANES), 1)
    lo = lane < GQA_HD
    one = (lane == MLA_V).astype(F32)

    cq = _rms(z[:, _Z_CQ[0]:_Z_CQ[1]], gqa_ref[...]).astype(BF16)
    qa = jnp.dot(cq, wqb_ref[...], preferred_element_type=F32)
    q_scale = (MLA_NOPE + MLA_ROPE) ** -0.5 * LOG2E
    for h in range(MLA_HEADS):
        sl = slice(h * LANES, (h + 1) * LANES)
        qm_ref[:, sl] = (_rope(qa[:, sl], mc, msa, msb, MLA_ROPE // 4) * q_scale).astype(BF16)

    ckv = _rms(z[:, _Z_CKV[0]:_Z_CKV[1]], gkv_ref[...]).astype(BF16)
    kv = jnp.dot(ckv, wkvb_ref[...], preferred_element_type=F32)
    kpe = _rope(z[:, _Z_KPE[0]:_Z_KPE[1]], mc, msa, msb, MLA_ROPE // 4)
    for h in range(MLA_HEADS):
        sl = slice(h * LANES, (h + 1) * LANES)
        km_ref[:, sl] = (kv[:, sl] + kpe).astype(BF16)
        vsl = slice((MLA_HEADS + h) * LANES, (MLA_HEADS + h + 1) * LANES)
        vm_ref[:, sl] = (kv[:, vsl] + one).astype(BF16)

    def head_mean_sq(v, gmat):
        sq = v * v
        hi = sq.astype(BF16)
        lo_part = (sq - hi.astype(F32)).astype(BF16)
        return (jnp.dot(hi, gmat, preferred_element_type=F32)
                + jnp.dot(lo_part, gmat, preferred_element_type=F32))

    qg = z[:, _Z_QG[0]:_Z_QG[1]]
    qg = qg * lax.rsqrt(head_mean_sq(qg, gmat_ref[...]) + EPS) * gq_ref[...]
    for j in range(GQA_HEADS // 2):
        sl = slice(j * LANES, (j + 1) * LANES)
        qg_ref[:, sl] = _rope(qg[:, sl], gc, gsa, gsb, GQA_HD // 4).astype(BF16)

    kg = z[:, _Z_KG[0]:_Z_KG[1]]
    kg = kg * lax.rsqrt(head_mean_sq(kg, gmat_ref[:LANES, :LANES]) + EPS) * gk_ref[...]
    kg = _rope(kg, gc, gsa, gsb, GQA_HD // 4)
    kg_sw = pltpu.roll(kg, GQA_HD, 1)
    zero = jnp.zeros_like(kg)
    kg_ref[:, 0 * LANES:1 * LANES] = jnp.where(lo, kg, zero).astype(BF16)
    kg_ref[:, 1 * LANES:2 * LANES] = jnp.where(lo, zero, kg_sw).astype(BF16)
    kg_ref[:, 2 * LANES:3 * LANES] = jnp.where(lo, kg_sw, zero).astype(BF16)
    kg_ref[:, 3 * LANES:4 * LANES] = jnp.where(lo, zero, kg).astype(BF16)

    vg = z[:, _Z_VG[0]:_Z_VG[1]]
    vg_sw = pltpu.roll(vg, GQA_HD, 1)
    vg_ref[:, 0 * LANES:1 * LANES] = jnp.where(lo, vg, one).astype(BF16)
    vg_ref[:, 1 * LANES:2 * LANES] = jnp.where(lo, vg_sw, one).astype(BF16)


def _prep_call(x, tab, seq, p):
    t = x.shape[0]
    tiles_per_seq = seq // ROW_TILE
    row = lambda w: pl.BlockSpec((ROW_TILE, w), lambda i: (i, 0))
    consts = [p["gmix"], p["win"], p["gqa"], p["wqb"], p["gkv"], p["wkvb"], p["gq"], p["gk"], p["gmat"]]
    widths = (MLA_HEADS * LANES,) * 3 + (GQA_HEADS // 2 * LANES, 2 * GQA_KV_HEADS * LANES, GQA_KV_HEADS * LANES)
    return pl.pallas_call(
        _prep_kernel,
        grid=(t // ROW_TILE,),
        in_specs=[row(D_MODEL), pl.BlockSpec((ROW_TILE, tab.shape[1]), lambda i: (i % tiles_per_seq, 0))]
        + [_const_spec(c.shape) for c in consts],
        out_specs=[row(w) for w in widths],
        out_shape=[jax.ShapeDtypeStruct((t, w), BF16) for w in widths],
        compiler_params=pltpu.CompilerParams(vmem_limit_bytes=VMEM_LIMIT_BYTES),
        name="prep",
    )(x, tab, *consts)


def _attn_kernel(q_ref, k_ref, v_ref, o_ref, vt_ref, sa_ref, sb_ref, acc_ref, qt_ref, *, q_slots, k_slots,
                 v_slots):
    seq = k_ref.shape[0]
    nq, nc, npairs = seq // Q_TILE, seq // KEY_CHUNK, len(q_slots) // 2
    tiles_per_iter = 2 if npairs % 2 else 1
    items_per_iter = npairs * tiles_per_iter
    n_iter = nq // tiles_per_iter
    assert len(q_slots) % 2 == 0 and nq % tiles_per_iter == 0
    s_bufs = (sa_ref, sb_ref)
    for slot in sorted(set(v_slots)):
        for c in range(nc):
            vt = v_ref[c * KEY_CHUNK:(c + 1) * KEY_CHUNK, slot * LANES:(slot + 1) * LANES].T
            vt_ref[slot, c] = vt[:PV_ROWS, :]
    neg_inf = [jnp.full((MAX_ROWS, Q_TILE), -jnp.inf, F32)] * 2

    def item(it, e):
        return it * tiles_per_iter + e // npairs, e % npairs, s_bufs[e % 2]

    def load_q(tile, pair):
        qrows = pl.ds(pl.multiple_of(tile * Q_TILE, Q_TILE), Q_TILE)
        for e in range(2):
            j = 2 * pair + e
            qt_ref[e] = q_ref[qrows, q_slots[j] * LANES:(q_slots[j] + 1) * LANES].T

    def score_chunk(pair, s_ref, c, mparts):
        krows = pl.ds(pl.multiple_of(c * KEY_CHUNK, KEY_CHUNK), KEY_CHUNK)
        out = []
        for e in range(2):
            j = 2 * pair + e
            k = k_ref[krows, k_slots[j] * LANES:(k_slots[j] + 1) * LANES]
            st = jnp.dot(k, qt_ref[e], preferred_element_type=F32)
            s_ref[e, krows, :] = st
            part = jnp.max(st.reshape(KEY_CHUNK // MAX_ROWS, MAX_ROWS, Q_TILE), axis=0)
            out.append(jnp.maximum(mparts[e], part))
        return out

    def pv_chunk(pair, s_ref, c, m):
        krows = pl.ds(pl.multiple_of(c * KEY_CHUNK, KEY_CHUNK), KEY_CHUNK)
        for e in range(2):
            pt = jnp.exp2((s_ref[e, krows, :] - m[e]).astype(BF16))
            acc_ref[e] += jnp.dot(vt_ref[v_slots[2 * pair + e], c], pt, preferred_element_type=F32)

    def col_max(mparts):
        return [jnp.max(mp, axis=0, keepdims=True) for mp in mparts]

    def finish(tile, pair):
        halves = [acc_ref[e, :MLA_V, :] / acc_ref[e, MLA_V:MLA_V + 1, :] for e in range(2)]
        ot = jnp.concatenate(halves, axis=0)
        rows = pl.ds(pl.multiple_of(tile * Q_TILE, Q_TILE), Q_TILE)
        o_ref[rows, pair * LANES:(pair + 1) * LANES] = ot.T.astype(BF16)
        acc_ref[...] = jnp.zeros_like(acc_ref)

    def item_step(it, e, m_prev):
        tile, pair, s_cur = item(it, e)
        tile_p, pair_p, s_prev = item(it, e - 1) if e > 0 else item(it - 1, items_per_iter - 1)
        load_q(tile, pair)

        def chunk(c, mp):
            mp = score_chunk(pair, s_cur, c, mp)
            pv_chunk(pair_p, s_prev, c, m_prev)
            return mp

        mparts = lax.fori_loop(0, nc, chunk, neg_inf, unroll=CHUNK_UNROLL)
        finish(tile_p, pair_p)
        return col_max(mparts)

    def trip(it, m, first):
        for e in range(first, items_per_iter):
            m = item_step(it, e, m)
        return m

    acc_ref[...] = jnp.zeros_like(acc_ref)
    tile0, pair0, s0 = item(0, 0)
    load_q(tile0, pair0)
    m = col_max(lax.fori_loop(0, nc, lambda c, mp: score_chunk(pair0, s0, c, mp), neg_inf, unroll=CHUNK_UNROLL))
    m = trip(0, m, 1)
    m = lax.fori_loop(1, n_iter, lambda it, m: trip(it, m, 0), m)

    tile_l, pair_l, s_l = item(n_iter - 1, items_per_iter - 1)

    def last_chunk(c, carry):
        pv_chunk(pair_l, s_l, c, m)
        return carry

    lax.fori_loop(0, nc, last_chunk, 0, unroll=CHUNK_UNROLL)
    finish(tile_l, pair_l)


def _attn_call(q, k, v, batch, seq, q_slots, k_slots, v_slots, name):
    nh = len(q_slots)
    qw, kw, vw = ((max(s) + 1) * LANES for s in (q_slots, k_slots, v_slots))
    ow = nh // 2 * LANES
    steps = q.shape[1] // qw
    assert k.shape[1] == steps * kw and v.shape[1] == steps * vw
    return pl.pallas_call(
        functools.partial(_attn_kernel, q_slots=q_slots, k_slots=k_slots, v_slots=v_slots),
        grid=(batch, steps),
        in_specs=[
            pl.BlockSpec((seq, qw), lambda b, p: (b, p)),
            pl.BlockSpec((seq, kw), lambda b, p: (b, p)),
            pl.BlockSpec((seq, vw), lambda b, p: (b, p)),
        ],
        out_specs=pl.BlockSpec((seq, ow), lambda b, p: (b, p)),
        out_shape=jax.ShapeDtypeStruct((q.shape[0], steps * ow), BF16),
        scratch_shapes=[
            pltpu.VMEM((vw // LANES, seq // KEY_CHUNK, PV_ROWS, KEY_CHUNK), BF16),
            pltpu.VMEM((2, seq, Q_TILE), F32),
            pltpu.VMEM((2, seq, Q_TILE), F32),
            pltpu.VMEM((2, PV_ROWS, Q_TILE), F32),
            pltpu.VMEM((2, LANES, Q_TILE), BF16),
        ],
        compiler_params=pltpu.CompilerParams(vmem_limit_bytes=VMEM_LIMIT_BYTES),
        name=name,
    )(q, k, v)


def _rope_tables(seq):
    rows = seq // GRID_W
    row = jnp.repeat(jnp.arange(rows, dtype=F32), GRID_W)
    col = jnp.tile(jnp.arange(GRID_W, dtype=F32), rows)

    def slot_tables(dim, lead, slot_w):
        half = dim // 2
        inv = ROPE_THETA ** (-jnp.arange(0, half, 2, dtype=F32) / half)
        ar, ac = row[:, None] * inv[None, :], col[:, None] * inv[None, :]
        z = jnp.zeros_like(ar)
        c = jnp.concatenate([jnp.cos(ar)] * 2 + [jnp.cos(ac)] * 2, axis=1)
        sa = jnp.concatenate([-jnp.sin(ar), z, -jnp.sin(ac), z], axis=1)
        sb = jnp.concatenate([z, jnp.sin(ar), z, jnp.sin(ac)], axis=1)
        tail = slot_w - lead - dim
        pad = lambda a, fill: jnp.concatenate(
            [jnp.full((seq, lead), fill, F32), a, jnp.full((seq, tail), fill, F32)], axis=1)
        return pad(c, 1.0), pad(sa, 0.0), pad(sb, 0.0)

    mla = slot_tables(MLA_ROPE, MLA_NOPE, LANES)
    gqa = [jnp.tile(a, (1, LANES // GQA_HD)) for a in slot_tables(GQA_HD, 0, GQA_HD)]
    return jnp.concatenate(list(mla) + gqa, axis=1)


def _layer_params(l, norm_ffn1, w_ffn1_gate, w_ffn1_up, w_ffn1_down, norm_mix, w_in, q_a_norm, w_q_b,
                  kv_a_norm, w_kv_b, gqa_q_norm, gqa_k_norm, w_out, norm_ffn2, w_ffn2_gate, w_ffn2_up,
                  w_ffn2_down):
    row = lambda v: v.reshape(1, -1).astype(F32)
    wi = w_in[l]
    o = np.cumsum([0, Q_LORA, KV_LORA, MLA_ROPE, GQA_HEADS * GQA_HD, GQA_KV_HEADS * GQA_HD, GQA_KV_HEADS * GQA_HD])
    zc = lambda n: jnp.zeros((D_MODEL, n), F32)
    win = jnp.concatenate(
        [wi[:, o[0]:o[1]], wi[:, o[1]:o[2]], zc(MLA_NOPE), wi[:, o[2]:o[3]], zc(LANES - MLA_NOPE - MLA_ROPE),
         wi[:, o[3]:o[4]], wi[:, o[4]:o[5]], wi[:, o[5]:o[6]]], axis=1).astype(BF16)
    assert win.shape[1] == _ZW

    wqb = w_q_b[l].reshape(Q_LORA, MLA_HEADS, MLA_NOPE + MLA_ROPE)
    wqb = jnp.pad(wqb, ((0, 0), (0, 0), (0, LANES - MLA_NOPE - MLA_ROPE))).reshape(Q_LORA, MLA_HEADS * LANES)

    wkv = w_kv_b[l].reshape(KV_LORA, MLA_HEADS, MLA_NOPE + MLA_V)
    wk = jnp.pad(wkv[:, :, :MLA_NOPE], ((0, 0), (0, 0), (0, LANES - MLA_NOPE)))
    wv = jnp.pad(wkv[:, :, MLA_NOPE:], ((0, 0), (0, 0), (0, LANES - MLA_V)))
    wkvb = jnp.concatenate([wk, wv], axis=1).reshape(KV_LORA, 2 * MLA_HEADS * LANES)

    gmat = np.kron(np.eye(GQA_HEADS), np.full((GQA_HD, GQA_HD), 1.0 / GQA_HD))
    return dict(
        g1=row(norm_ffn1[l]), wg1=w_ffn1_gate[l].astype(BF16), wu1=w_ffn1_up[l].astype(BF16),
        wd1=w_ffn1_down[l].astype(BF16),
        gmix=row(norm_mix[l]), win=win, gqa=row(q_a_norm[l]), wqb=wqb.astype(BF16), gkv=row(kv_a_norm[l]),
        wkvb=wkvb.astype(BF16),
        gq=row(jnp.tile(gqa_q_norm[l], GQA_HEADS)) * (GQA_HD ** -0.5 * LOG2E),
        gk=row(jnp.tile(gqa_k_norm[l], GQA_KV_HEADS)),
        gmat=jnp.asarray(gmat, BF16), wout=w_out[l].astype(BF16),
        g2=row(norm_ffn2[l]), wg2=w_ffn2_gate[l].astype(BF16), wu2=w_ffn2_up[l].astype(BF16),
        wd2=w_ffn2_down[l].astype(BF16),
    )


def _trunk(x3, layers, tab, final_g):
    batch, seq, _ = x3.shape
    assert seq % ROW_TILE == 0 and seq % Q_TILE == 0 and seq % GRID_W == 0
    x = x3.reshape(batch * seq, D_MODEL)
    for li, p in enumerate(layers):
        x = _ffn_call(x, p["g1"], p["wg1"], p["wu1"], p["wd1"])
        qm, km, vm, qg, kg, vg = _prep_call(x, tab, seq, p)
        o_mla = _attn_call(qm, km, vm, batch, seq, (0, 1), (0, 1), (0, 1), "attn_mla")
        o_gqa = _attn_call(qg, kg, vg, batch, seq, (0, 0, 1, 1), (0, 1, 0, 1), (0, 0, 0, 0), "attn_gqa")
        x = _ffn_call(x, p["g2"], p["wg2"], p["wu2"], p["wd2"], attn=(o_mla, o_gqa, p["wout"]),
                      final_g=final_g if li == len(layers) - 1 else None)
    return x.reshape(batch, seq, D_MODEL)


def kernel(x_prompt, x_sample, norm_ffn1, w_ffn1_gate, w_ffn1_up, w_ffn1_down, norm_mix, w_in, q_a_norm, w_q_b, kv_a_norm, w_kv_b, gqa_q_norm, gqa_k_norm, w_out, norm_ffn2, w_ffn2_gate, w_ffn2_up, w_ffn2_down, final_norm):
    depth = norm_ffn1.shape[0]
    layers = [
        _layer_params(l, norm_ffn1, w_ffn1_gate, w_ffn1_up, w_ffn1_down, norm_mix, w_in, q_a_norm, w_q_b,
                      kv_a_norm, w_kv_b, gqa_q_norm, gqa_k_norm, w_out, norm_ffn2, w_ffn2_gate, w_ffn2_up,
                      w_ffn2_down)
        for l in range(depth)
    ]
    final_g = final_norm.reshape(1, -1).astype(F32)
    outs = []
    for x3 in (x_prompt, x_sample):
        tab = _rope_tables(x3.shape[1])
        outs.append(_trunk(x3, layers, tab, final_g))
    return tuple(outs)
```

```python
import functools

import jax
import jax.numpy as jnp
import numpy as np
from jax import lax
from jax.experimental import pallas as pl
from jax.experimental.pallas import tpu as pltpu

D_MODEL = 1024
GRID_W = 64
ROPE_THETA = 10000.0
EPS = 1e-6

MLA_HEADS = 8
MLA_NOPE = 64
MLA_ROPE = 32
MLA_V = 64
Q_LORA = 256
KV_LORA = 128
GQA_HEADS = 8
GQA_KV_HEADS = 2
GQA_HD = 64
D_FF = 2816

LANES = 128
LOG2E = 1.4426950408889634

ROW_TILE = 1024
FF_CHUNK = 256
Q_TILE = 512
KEY_CHUNK = 512
PV_ROWS = 128
MAX_ROWS = 32
CHUNK_UNROLL = 4
VMEM_LIMIT_BYTES = 56 * 1024 * 1024

_ZW = Q_LORA + KV_LORA + LANES + GQA_HEADS * GQA_HD + 2 * LANES
_Z_CQ = (0, Q_LORA)
_Z_CKV = (Q_LORA, Q_LORA + KV_LORA)
_Z_KPE = (_Z_CKV[1], _Z_CKV[1] + LANES)
_Z_QG = (_Z_KPE[1], _Z_KPE[1] + GQA_HEADS * GQA_HD)
_Z_KG = (_Z_QG[1], _Z_QG[1] + LANES)
_Z_VG = (_Z_KG[1], _Z_KG[1] + LANES)

BF16 = jnp.bfloat16
F32 = jnp.float32


def _rms(x, g):
    ms = jnp.mean(x * x, axis=-1, keepdims=True)
    return x * lax.rsqrt(ms + EPS) * g


def _rope(x, c, sa, sb, d):
    return x * c + pltpu.roll(x, LANES - d, 1) * sa + pltpu.roll(x, d, 1) * sb


def _ffn_kernel(*refs, has_attn, final):
    it = iter(refs)
    x_ref = next(it)
    if has_attn:
        om_ref, og_ref, wo_ref = next(it), next(it), next(it)
    g_ref, wg_ref, wu_ref, wd_ref = next(it), next(it), next(it), next(it)
    if final:
        fg_ref = next(it)
    o_ref = next(it)
    a_ref = next(it)

    x = x_ref[...]
    if has_attn:
        half = om_ref.shape[1]
        x = x + jnp.dot(om_ref[...], wo_ref[:half, :], preferred_element_type=F32)
        x = x + jnp.dot(og_ref[...], wo_ref[half:, :], preferred_element_type=F32)
    hb = _rms(x, g_ref[...]).astype(BF16)
    for c in range(D_FF // FF_CHUNK):
        sl = slice(c * FF_CHUNK, (c + 1) * FF_CHUNK)
        g = jnp.dot(hb, wg_ref[:, sl], preferred_element_type=F32)
        u = jnp.dot(hb, wu_ref[:, sl], preferred_element_type=F32)
        a_ref[:, sl] = (g * jax.nn.sigmoid(g) * u).astype(BF16)
    y = x + 0.5 * jnp.dot(a_ref[...], wd_ref[...], preferred_element_type=F32)
    if final:
        y = _rms(y, fg_ref[...])
    o_ref[...] = y


def _const_spec(shape):
    return pl.BlockSpec(shape, lambda *_: (0,) * len(shape), pipeline_mode=pl.Buffered(1))


def _ffn_call(x, g, wg, wu, wd, attn=None, final_g=None):
    t = x.shape[0]
    row = lambda w: pl.BlockSpec((ROW_TILE, w), lambda i: (i, 0))
    args, specs = [x], [row(D_MODEL)]
    if attn is not None:
        o_mla, o_gqa, w_out = attn
        args += [o_mla, o_gqa, w_out]
        specs += [row(o_mla.shape[1]), row(o_gqa.shape[1]), _const_spec(w_out.shape)]
    args += [g, wg, wu, wd]
    specs += [_const_spec(a.shape) for a in (g, wg, wu, wd)]
    if final_g is not None:
        args.append(final_g)
        specs.append(_const_spec(final_g.shape))
    return pl.pallas_call(
        functools.partial(_ffn_kernel, has_attn=attn is not None, final=final_g is not None),
        grid=(t // ROW_TILE,),
        in_specs=specs,
        out_specs=row(D_MODEL),
        out_shape=jax.ShapeDtypeStruct((t, D_MODEL), F32),
        scratch_shapes=[pltpu.VMEM((ROW_TILE, D_FF), BF16)],
        compiler_params=pltpu.CompilerParams(vmem_limit_bytes=VMEM_LIMIT_BYTES),
        name="ffn_attn" if attn is not None else "ffn",
    )(*args)


def _prep_kernel(x_ref, tab_ref, gmix_ref, win_ref, gqa_ref, wqb_ref, gkv_ref, wkvb_ref, gq_ref, gk_ref,
                 gmat_ref, qm_ref, km_ref, vm_ref, qg_ref, kg_ref, vg_ref):
    hb = _rms(x_ref[...], gmix_ref[...]).astype(BF16)
    z = jnp.dot(hb, win_ref[...], preferred_element_type=F32)

    mc, msa, msb = (tab_ref[:, j * LANES:(j + 1) * LANES] for j in range(3))
    gc, gsa, gsb = (tab_ref[:, j * LANES:(j + 1) * LANES] for j in range(3, 6))
    lane = lax.broadcasted_iota(jnp.int32, (1, LANES), 1)
    lo = lane < GQA_HD
    one = (lane == MLA_V).astype(F32)

    cq = _rms(z[:, _Z_CQ[0]:_Z_CQ[1]], gqa_ref[...]).astype(BF16)
    qa = jnp.dot(cq, wqb_ref[...], preferred_element_type=F32)
    q_scale = (MLA_NOPE + MLA_ROPE) ** -0.5 * LOG2E
    for h in range(MLA_HEADS):
        sl = slice(h * LANES, (h + 1) * LANES)
        qm_ref[:, sl] = (_rope(qa[:, sl], mc, msa, msb, MLA_ROPE // 4) * q_scale).astype(BF16)

    ckv = _rms(z[:, _Z_CKV[0]:_Z_CKV[1]], gkv_ref[...]).astype(BF16)
    kv = jnp.dot(ckv, wkvb_ref[...], preferred_element_type=F32)
    kpe = _rope(z[:, _Z_KPE[0]:_Z_KPE[1]], mc, msa, msb, MLA_ROPE // 4)
    for h in range(MLA_HEADS):
        sl = slice(h * LANES, (h + 1) * LANES)
        km_ref[:, sl] = (kv[:, sl] + kpe).astype(BF16)
        vsl = slice((MLA_HEADS + h) * LANES, (MLA_HEADS + h + 1) * LANES)
        vm_ref[:, sl] = (kv[:, vsl] + one).astype(BF16)

    def head_mean_sq(v, gmat):
        sq = v * v
        hi = sq.astype(BF16)
        lo_part = (sq - hi.astype(F32)).astype(BF16)
        return (jnp.dot(hi, gmat, preferred_element_type=F32)
                + jnp.dot(lo_part, gmat, preferred_element_type=F32))

    qg = z[:, _Z_QG[0]:_Z_QG[1]]
    qg = qg * lax.rsqrt(head_mean_sq(qg, gmat_ref[...]) + EPS) * gq_ref[...]
    for j in range(GQA_HEADS // 2):
        sl = slice(j * LANES, (j + 1) * LANES)
        qg_ref[:, sl] = _rope(qg[:, sl], gc, gsa, gsb, GQA_HD // 4).astype(BF16)

    kg = z[:, _Z_KG[0]:_Z_KG[1]]
    kg = kg * lax.rsqrt(head_mean_sq(kg, gmat_ref[:LANES, :LANES]) + EPS) * gk_ref[...]
    kg = _rope(kg, gc, gsa, gsb, GQA_HD // 4)
    kg_sw = pltpu.roll(kg, GQA_HD, 1)
    zero = jnp.zeros_like(kg)
    kg_ref[:, 0 * LANES:1 * LANES] = jnp.where(lo, kg, zero).astype(BF16)
    kg_ref[:, 1 * LANES:2 * LANES] = jnp.where(lo, zero, kg_sw).astype(BF16)
    kg_ref[:, 2 * LANES:3 * LANES] = jnp.where(lo, kg_sw, zero).astype(BF16)
    kg_ref[:, 3 * LANES:4 * LANES] = jnp.where(lo, zero, kg).astype(BF16)

    vg = z[:, _Z_VG[0]:_Z_VG[1]]
    vg_sw = pltpu.roll(vg, GQA_HD, 1)
    vg_ref[:, 0 * LANES:1 * LANES] = jnp.where(lo, vg, one).astype(BF16)
    vg_ref[:, 1 * LANES:2 * LANES] = jnp.where(lo, vg_sw, one).astype(BF16)


def _prep_call(x, tab, seq, p):
    t = x.shape[0]
    tiles_per_seq = seq // ROW_TILE
    row = lambda w: pl.BlockSpec((ROW_TILE, w), lambda i: (i, 0))
    consts = [p["gmix"], p["win"], p["gqa"], p["wqb"], p["gkv"], p["wkvb"], p["gq"], p["gk"], p["gmat"]]
    widths = (MLA_HEADS * LANES,) * 3 + (GQA_HEADS // 2 * LANES, 2 * GQA_KV_HEADS * LANES, GQA_KV_HEADS * LANES)
    return pl.pallas_call(
        _prep_kernel,
        grid=(t // ROW_TILE,),
        in_specs=[row(D_MODEL), pl.BlockSpec((ROW_TILE, tab.shape[1]), lambda i: (i % tiles_per_seq, 0))]
        + [_const_spec(c.shape) for c in consts],
        out_specs=[row(w) for w in widths],
        out_shape=[jax.ShapeDtypeStruct((t, w), BF16) for w in widths],
        compiler_params=pltpu.CompilerParams(vmem_limit_bytes=VMEM_LIMIT_BYTES),
        name="prep",
    )(x, tab, *consts)


def _attn_kernel(q_ref, k_ref, v_ref, o_ref, vt_ref, sa_ref, sb_ref, acc_ref, *, q_slots, k_slots, v_slots):
    seq = k_ref.shape[0]
    nq, nc, npairs = seq // Q_TILE, seq // KEY_CHUNK, len(q_slots) // 2
    tiles_per_iter = 2 if npairs % 2 else 1
    items_per_iter = npairs * tiles_per_iter
    n_iter = nq // tiles_per_iter
    assert len(q_slots) % 2 == 0 and nq % tiles_per_iter == 0
    s_bufs = (sa_ref, sb_ref)
    for slot in sorted(set(v_slots)):
        for c in range(nc):
            vt = v_ref[c * KEY_CHUNK:(c + 1) * KEY_CHUNK, slot * LANES:(slot + 1) * LANES].T
            vt_ref[slot, c] = vt[:PV_ROWS, :]
    neg_inf = [jnp.full((MAX_ROWS, Q_TILE), -jnp.inf, F32)] * 2

    def item(it, e):
        return it * tiles_per_iter + e // npairs, e % npairs, s_bufs[e % 2]

    def score_chunk(tile, pair, s_ref, c, mparts):
        qrows = pl.ds(pl.multiple_of(tile * Q_TILE, Q_TILE), Q_TILE)
        krows = pl.ds(pl.multiple_of(c * KEY_CHUNK, KEY_CHUNK), KEY_CHUNK)
        out = []
        for e in range(2):
            j = 2 * pair + e
            q = q_ref[qrows, q_slots[j] * LANES:(q_slots[j] + 1) * LANES]
            k = k_ref[krows, k_slots[j] * LANES:(k_slots[j] + 1) * LANES]
            st = lax.dot_general(k, q, (((1,), (1,)), ((), ())), preferred_element_type=F32)
            s_ref[e, krows, :] = st
            part = jnp.max(st.reshape(KEY_CHUNK // MAX_ROWS, MAX_ROWS, Q_TILE), axis=0)
            out.append(jnp.maximum(mparts[e], part))
        return out

    def pv_chunk(pair, s_ref, c, m):
        krows = pl.ds(pl.multiple_of(c * KEY_CHUNK, KEY_CHUNK), KEY_CHUNK)
        for e in range(2):
            pt = jnp.exp2((s_ref[e, krows, :] - m[e]).astype(BF16))
            acc_ref[e] += jnp.dot(vt_ref[v_slots[2 * pair + e], c], pt, preferred_element_type=F32)

    def col_max(mparts):
        return [jnp.max(mp, axis=0, keepdims=True) for mp in mparts]

    def finish(tile, pair):
        halves = [acc_ref[e, :MLA_V, :] / acc_ref[e, MLA_V:MLA_V + 1, :] for e in range(2)]
        ot = jnp.concatenate(halves, axis=0)
        rows = pl.ds(pl.multiple_of(tile * Q_TILE, Q_TILE), Q_TILE)
        o_ref[rows, pair * LANES:(pair + 1) * LANES] = ot.T.astype(BF16)
        acc_ref[...] = jnp.zeros_like(acc_ref)

    def item_step(it, e, m_prev):
        tile, pair, s_cur = item(it, e)
        tile_p, pair_p, s_prev = item(it, e - 1) if e > 0 else item(it - 1, items_per_iter - 1)

        def chunk(c, mp):
            mp = score_chunk(tile, pair, s_cur, c, mp)
            pv_chunk(pair_p, s_prev, c, m_prev)
            return mp

        mparts = lax.fori_loop(0, nc, chunk, neg_inf, unroll=CHUNK_UNROLL)
        finish(tile_p, pair_p)
        return col_max(mparts)

    def trip(it, m, first):
        for e in range(first, items_per_iter):
            m = item_step(it, e, m)
        return m

    acc_ref[...] = jnp.zeros_like(acc_ref)
    tile0, pair0, s0 = item(0, 0)
    m = col_max(lax.fori_loop(0, nc, lambda c, mp: score_chunk(tile0, pair0, s0, c, mp), neg_inf,
                              unroll=CHUNK_UNROLL))
    m = trip(0, m, 1)
    m = lax.fori_loop(1, n_iter, lambda it, m: trip(it, m, 0), m)

    tile_l, pair_l, s_l = item(n_iter - 1, items_per_iter - 1)

    def last_chunk(c, carry):
        pv_chunk(pair_l, s_l, c, m)
        return carry

    lax.fori_loop(0, nc, last_chunk, 0, unroll=CHUNK_UNROLL)
    finish(tile_l, pair_l)


def _attn_call(q, k, v, batch, seq, q_slots, k_slots, v_slots, name):
    nh = len(q_slots)
    qw, kw, vw = ((max(s) + 1) * LANES for s in (q_slots, k_slots, v_slots))
    ow = nh // 2 * LANES
    steps = q.shape[1] // qw
    assert k.shape[1] == steps * kw and v.shape[1] == steps * vw
    return pl.pallas_call(
        functools.partial(_attn_kernel, q_slots=q_slots, k_slots=k_slots, v_slots=v_slots),
        grid=(batch, steps),
        in_specs=[
            pl.BlockSpec((seq, qw), lambda b, p: (b, p)),
            pl.BlockSpec((seq, kw), lambda b, p: (b, p)),
            pl.BlockSpec((seq, vw), lambda b, p: (b, p)),
        ],
        out_specs=pl.BlockSpec((seq, ow), lambda b, p: (b, p)),
        out_shape=jax.ShapeDtypeStruct((q.shape[0], steps * ow), BF16),
        scratch_shapes=[
            pltpu.VMEM((vw // LANES, seq // KEY_CHUNK, PV_ROWS, KEY_CHUNK), BF16),
            pltpu.VMEM((2, seq, Q_TILE), F32),
            pltpu.VMEM((2, seq, Q_TILE), F32),
            pltpu.VMEM((2, PV_ROWS, Q_TILE), F32),
        ],
        compiler_params=pltpu.CompilerParams(vmem_limit_bytes=VMEM_LIMIT_BYTES),
        name=name,
    )(q, k, v)


def _rope_tables(seq):
    rows = seq // GRID_W
    row = jnp.repeat(jnp.arange(rows, dtype=F32), GRID_W)
    col = jnp.tile(jnp.arange(GRID_W, dtype=F32), rows)

    def slot_tables(dim, lead, slot_w):
        half = dim // 2
        inv = ROPE_THETA ** (-jnp.arange(0, half, 2, dtype=F32) / half)
        ar, ac = row[:, None] * inv[None, :], col[:, None] * inv[None, :]
        z = jnp.zeros_like(ar)
        c = jnp.concatenate([jnp.cos(ar)] * 2 + [jnp.cos(ac)] * 2, axis=1)
        sa = jnp.concatenate([-jnp.sin(ar), z, -jnp.sin(ac), z], axis=1)
        sb = jnp.concatenate([z, jnp.sin(ar), z, jnp.sin(ac)], axis=1)
        tail = slot_w - lead - dim
        pad = lambda a, fill: jnp.concatenate(
            [jnp.full((seq, lead), fill, F32), a, jnp.full((seq, tail), fill, F32)], axis=1)
        return pad(c, 1.0), pad(sa, 0.0), pad(sb, 0.0)

    mla = slot_tables(MLA_ROPE, MLA_NOPE, LANES)
    gqa = [jnp.tile(a, (1, LANES // GQA_HD)) for a in slot_tables(GQA_HD, 0, GQA_HD)]
    return jnp.concatenate(list(mla) + gqa, axis=1)


def _layer_params(l, norm_ffn1, w_ffn1_gate, w_ffn1_up, w_ffn1_down, norm_mix, w_in, q_a_norm, w_q_b,
                  kv_a_norm, w_kv_b, gqa_q_norm, gqa_k_norm, w_out, norm_ffn2, w_ffn2_gate, w_ffn2_up,
                  w_ffn2_down):
    row = lambda v: v.reshape(1, -1).astype(F32)
    wi = w_in[l]
    o = np.cumsum([0, Q_LORA, KV_LORA, MLA_ROPE, GQA_HEADS * GQA_HD, GQA_KV_HEADS * GQA_HD, GQA_KV_HEADS * GQA_HD])
    zc = lambda n: jnp.zeros((D_MODEL, n), F32)
    win = jnp.concatenate(
        [wi[:, o[0]:o[1]], wi[:, o[1]:o[2]], zc(MLA_NOPE), wi[:, o[2]:o[3]], zc(LANES - MLA_NOPE - MLA_ROPE),
         wi[:, o[3]:o[4]], wi[:, o[4]:o[5]], wi[:, o[5]:o[6]]], axis=1).astype(BF16)
    assert win.shape[1] == _ZW

    wqb = w_q_b[l].reshape(Q_LORA, MLA_HEADS, MLA_NOPE + MLA_ROPE)
    wqb = jnp.pad(wqb, ((0, 0), (0, 0), (0, LANES - MLA_NOPE - MLA_ROPE))).reshape(Q_LORA, MLA_HEADS * LANES)

    wkv = w_kv_b[l].reshape(KV_LORA, MLA_HEADS, MLA_NOPE + MLA_V)
    wk = jnp.pad(wkv[:, :, :MLA_NOPE], ((0, 0), (0, 0), (0, LANES - MLA_NOPE)))
    wv = jnp.pad(wkv[:, :, MLA_NOPE:], ((0, 0), (0, 0), (0, LANES - MLA_V)))
    wkvb = jnp.concatenate([wk, wv], axis=1).reshape(KV_LORA, 2 * MLA_HEADS * LANES)

    gmat = np.kron(np.eye(GQA_HEADS), np.full((GQA_HD, GQA_HD), 1.0 / GQA_HD))
    return dict(
        g1=row(norm_ffn1[l]), wg1=w_ffn1_gate[l].astype(BF16), wu1=w_ffn1_up[l].astype(BF16),
        wd1=w_ffn1_down[l].astype(BF16),
        gmix=row(norm_mix[l]), win=win, gqa=row(q_a_norm[l]), wqb=wqb.astype(BF16), gkv=row(kv_a_norm[l]),
        wkvb=wkvb.astype(BF16),
        gq=row(jnp.tile(gqa_q_norm[l], GQA_HEADS)) * (GQA_HD ** -0.5 * LOG2E),
        gk=row(jnp.tile(gqa_k_norm[l], GQA_KV_HEADS)),
        gmat=jnp.asarray(gmat, BF16), wout=w_out[l].astype(BF16),
        g2=row(norm_ffn2[l]), wg2=w_ffn2_gate[l].astype(BF16), wu2=w_ffn2_up[l].astype(BF16),
        wd2=w_ffn2_down[l].astype(BF16),
    )


def _trunk(x3, layers, tab, final_g):
    batch, seq, _ = x3.shape
    assert seq % ROW_TILE == 0 and seq % Q_TILE == 0 and seq % GRID_W == 0
    x = x3.reshape(batch * seq, D_MODEL)
    for li, p in enumerate(layers):
        x = _ffn_call(x, p["g1"], p["wg1"], p["wu1"], p["wd1"])
        qm, km, vm, qg, kg, vg = _prep_call(x, tab, seq, p)
        o_mla = _attn_call(qm, km, vm, batch, seq, (0, 1), (0, 1), (0, 1), "attn_mla")
        o_gqa = _attn_call(qg, kg, vg, batch, seq, (0, 0, 1, 1), (0, 1, 0, 1), (0, 0, 0, 0), "attn_gqa")
        x = _ffn_call(x, p["g2"], p["wg2"], p["wu2"], p["wd2"], attn=(o_mla, o_gqa, p["wout"]),
                      final_g=final_g if li == len(layers) - 1 else None)
    return x.reshape(batch, seq, D_MODEL)


def kernel(x_prompt, x_sample, norm_ffn1, w_ffn1_gate, w_ffn1_up, w_ffn1_down, norm_mix, w_in, q_a_norm, w_q_b, kv_a_norm, w_kv_b, gqa_q_norm, gqa_k_norm, w_out, norm_ffn2, w_ffn2_gate, w_ffn2_up, w_ffn2_down, final_norm):
    depth = norm_ffn1.shape[0]
    layers = [
        _layer_params(l, norm_ffn1, w_ffn1_gate, w_ffn1_up, w_ffn1_down, norm_mix, w_in, q_a_norm, w_q_b,
                      kv_a_norm, w_kv_b, gqa_q_norm, gqa_k_norm, w_out, norm_ffn2, w_ffn2_gate, w_ffn2_up,
                      w_ffn2_down)
        for l in range(depth)
    ]
    final_g = final_norm.reshape(1, -1).astype(F32)
    outs = []
    for x3 in (x_prompt, x_sample):
        tab = _rope_tables(x3.shape[1])
        outs.append(_trunk(x3, layers, tab, final_g))
    return tuple(outs)
```

```python
import functools

import jax
import jax.numpy as jnp
import numpy as np
from jax import lax
from jax.experimental import pallas as pl
from jax.experimental.pallas import tpu as pltpu

D_MODEL = 1024
GRID_W = 64
ROPE_THETA = 10000.0
EPS = 1e-6

MLA_HEADS = 8
MLA_NOPE = 64
MLA_ROPE = 32
MLA_V = 64
Q_LORA = 256
KV_LORA = 128
GQA_HEADS = 8
GQA_KV_HEADS = 2
GQA_HD = 64
D_FF = 2816

LANES = 128
LOG2E = 1.4426950408889634

ROW_TILE = 1024
FF_CHUNK = 256
Q_TILE = 512
KEY_CHUNK = 512
PV_ROWS = 128
MAX_ROWS = 32
CHUNK_UNROLL = 4
VMEM_LIMIT_BYTES = 56 * 1024 * 1024

_ZW = Q_LORA + KV_LORA + LANES + GQA_HEADS * GQA_HD + 2 * LANES
_Z_CQ = (0, Q_LORA)
_Z_CKV = (Q_LORA, Q_LORA + KV_LORA)
_Z_KPE = (_Z_CKV[1], _Z_CKV[1] + LANES)
_Z_QG = (_Z_KPE[1], _Z_KPE[1] + GQA_HEADS * GQA_HD)
_Z_KG = (_Z_QG[1], _Z_QG[1] + LANES)
_Z_VG = (_Z_KG[1], _Z_KG[1] + LANES)

BF16 = jnp.bfloat16
F32 = jnp.float32


def _rms(x, g):
    ms = jnp.mean(x * x, axis=-1, keepdims=True)
    return x * lax.rsqrt(ms + EPS) * g


def _rope(x, c, sa, sb, d):
    return x * c + pltpu.roll(x, LANES - d, 1) * sa + pltpu.roll(x, d, 1) * sb


def _ffn_kernel(*refs, has_attn, final):
    it = iter(refs)
    x_ref = next(it)
    if has_attn:
        om_ref, og_ref, wo_ref = next(it), next(it), next(it)
    g_ref, wg_ref, wu_ref, wd_ref = next(it), next(it), next(it), next(it)
    if final:
        fg_ref = next(it)
    o_ref = next(it)
    a_ref = next(it)

    x = x_ref[...]
    if has_attn:
        half = om_ref.shape[1]
        x = x + jnp.dot(om_ref[...], wo_ref[:half, :], preferred_element_type=F32)
        x = x + jnp.dot(og_ref[...], wo_ref[half:, :], preferred_element_type=F32)
    hb = _rms(x, g_ref[...]).astype(BF16)
    for c in range(D_FF // FF_CHUNK):
        sl = slice(c * FF_CHUNK, (c + 1) * FF_CHUNK)
        g = jnp.dot(hb, wg_ref[:, sl], preferred_element_type=F32)
        u = jnp.dot(hb, wu_ref[:, sl], preferred_element_type=F32)
        a_ref[:, sl] = (g * jax.nn.sigmoid(g) * u).astype(BF16)
    y = x + 0.5 * jnp.dot(a_ref[...], wd_ref[...], preferred_element_type=F32)
    if final:
        y = _rms(y, fg_ref[...])
    o_ref[...] = y


def _const_spec(shape):
    return pl.BlockSpec(shape, lambda *_: (0,) * len(shape), pipeline_mode=pl.Buffered(1))


def _ffn_call(x, g, wg, wu, wd, attn=None, final_g=None):
    t = x.shape[0]
    row = lambda w: pl.BlockSpec((ROW_TILE, w), lambda i: (i, 0))
    args, specs = [x], [row(D_MODEL)]
    if attn is not None:
        o_mla, o_gqa, w_out = attn
        args += [o_mla, o_gqa, w_out]
        specs += [row(o_mla.shape[1]), row(o_gqa.shape[1]), _const_spec(w_out.shape)]
    args += [g, wg, wu, wd]
    specs += [_const_spec(a.shape) for a in (g, wg, wu, wd)]
    if final_g is not None:
        args.append(final_g)
        specs.append(_const_spec(final_g.shape))
    return pl.pallas_call(
        functools.partial(_ffn_kernel, has_attn=attn is not None, final=final_g is not None),
        grid=(t // ROW_TILE,),
        in_specs=specs,
        out_specs=row(D_MODEL),
        out_shape=jax.ShapeDtypeStruct((t, D_MODEL), F32),
        scratch_shapes=[pltpu.VMEM((ROW_TILE, D_FF), BF16)],
        compiler_params=pltpu.CompilerParams(vmem_limit_bytes=VMEM_LIMIT_BYTES),
        name="ffn_attn" if attn is not None else "ffn",
    )(*args)


def _prep_kernel(x_ref, tab_ref, gmix_ref, win_ref, gqa_ref, wqb_ref, gkv_ref, wkvb_ref, gq_ref, gk_ref,
                 gmat_ref, qm_ref, km_ref, vm_ref, qg_ref, kg_ref, vg_ref):
    hb = _rms(x_ref[...], gmix_ref[...]).astype(BF16)
    z = jnp.dot(hb, win_ref[...], preferred_element_type=F32)

    mc, msa, msb = (tab_ref[:, j * LANES:(j + 1) * LANES] for j in range(3))
    gc, gsa, gsb = (tab_ref[:, j * LANES:(j + 1) * LANES] for j in range(3, 6))
    lane = lax.broadcasted_iota(jnp.int32, (1, LANES), 1)
    lo = lane < GQA_HD
    one = (lane == MLA_V).astype(F32)

    cq = _rms(z[:, _Z_CQ[0]:_Z_CQ[1]], gqa_ref[...]).astype(BF16)
    qa = jnp.dot(cq, wqb_ref[...], preferred_element_type=F32)
    q_scale = (MLA_NOPE + MLA_ROPE) ** -0.5 * LOG2E
    for h in range(MLA_HEADS):
        sl = slice(h * LANES, (h + 1) * LANES)
        qm_ref[:, sl] = (_rope(qa[:, sl], mc, msa, msb, MLA_ROPE // 4) * q_scale).astype(BF16)

    ckv = _rms(z[:, _Z_CKV[0]:_Z_CKV[1]], gkv_ref[...]).astype(BF16)
    kv = jnp.dot(ckv, wkvb_ref[...], preferred_element_type=F32)
    kpe = _rope(z[:, _Z_KPE[0]:_Z_KPE[1]], mc, msa, msb, MLA_ROPE // 4)
    for h in range(MLA_HEADS):
        sl = slice(h * LANES, (h + 1) * LANES)
        km_ref[:, sl] = (kv[:, sl] + kpe).astype(BF16)
        vsl = slice((MLA_HEADS + h) * LANES, (MLA_HEADS + h + 1) * LANES)
        vm_ref[:, sl] = (kv[:, vsl] + one).astype(BF16)

    def head_mean_sq(v, gmat):
        sq = v * v
        hi = sq.astype(BF16)
        lo_part = (sq - hi.astype(F32)).astype(BF16)
        return (jnp.dot(hi, gmat, preferred_element_type=F32)
                + jnp.dot(lo_part, gmat, preferred_element_type=F32))

    qg = z[:, _Z_QG[0]:_Z_QG[1]]
    qg = qg * lax.rsqrt(head_mean_sq(qg, gmat_ref[...]) + EPS) * gq_ref[...]
    for j in range(GQA_HEADS // 2):
        sl = slice(j * LANES, (j + 1) * LANES)
        qg_ref[:, sl] = _rope(qg[:, sl], gc, gsa, gsb, GQA_HD // 4).astype(BF16)

    kg = z[:, _Z_KG[0]:_Z_KG[1]]
    kg = kg * lax.rsqrt(head_mean_sq(kg, gmat_ref[:LANES, :LANES]) + EPS) * gk_ref[...]
    kg = _rope(kg, gc, gsa, gsb, GQA_HD // 4)
    kg_sw = pltpu.roll(kg, GQA_HD, 1)
    zero = jnp.zeros_like(kg)
    kg_ref[:, 0 * LANES:1 * LANES] = jnp.where(lo, kg, zero).astype(BF16)
    kg_ref[:, 1 * LANES:2 * LANES] = jnp.where(lo, zero, kg_sw).astype(BF16)
    kg_ref[:, 2 * LANES:3 * LANES] = jnp.where(lo, kg_sw, zero).astype(BF16)
    kg_ref[:, 3 * LANES:4 * LANES] = jnp.where(lo, zero, kg).astype(BF16)

    vg = z[:, _Z_VG[0]:_Z_VG[1]]
    vg_sw = pltpu.roll(vg, GQA_HD, 1)
    vg_ref[:, 0 * LANES:1 * LANES] = jnp.where(lo, vg, one).astype(BF16)
    vg_ref[:, 1 * LANES:2 * LANES] = jnp.where(lo, vg_sw, one).astype(BF16)


def _prep_call(x, tab, seq, p):
    t = x.shape[0]
    tiles_per_seq = seq // ROW_TILE
    row = lambda w: pl.BlockSpec((ROW_TILE, w), lambda i: (i, 0))
    consts = [p["gmix"], p["win"], p["gqa"], p["wqb"], p["gkv"], p["wkvb"], p["gq"], p["gk"], p["gmat"]]
    widths = (MLA_HEADS * LANES,) * 3 + (GQA_HEADS // 2 * LANES, 2 * GQA_KV_HEADS * LANES, GQA_KV_HEADS * LANES)
    return pl.pallas_call(
        _prep_kernel,
        grid=(t // ROW_TILE,),
        in_specs=[row(D_MODEL), pl.BlockSpec((ROW_TILE, tab.shape[1]), lambda i: (i % tiles_per_seq, 0))]
        + [_const_spec(c.shape) for c in consts],
        out_specs=[row(w) for w in widths],
        out_shape=[jax.ShapeDtypeStruct((t, w), BF16) for w in widths],
        compiler_params=pltpu.CompilerParams(vmem_limit_bytes=VMEM_LIMIT_BYTES),
        name="prep",
    )(x, tab, *consts)


def _attn_kernel(q_ref, k_ref, v_ref, o_ref, vt_ref, sa_ref, sb_ref, acc_ref, *, q_slots, k_slots, v_slots):
    seq = k_ref.shape[0]
    nq, nc, npairs = seq // Q_TILE, seq // KEY_CHUNK, len(q_slots) // 2
    tiles_per_iter = 2 if npairs % 2 else 1
    items_per_iter = npairs * tiles_per_iter
    n_iter = nq // tiles_per_iter
    assert len(q_slots) % 2 == 0 and nq % tiles_per_iter == 0
    s_bufs = (sa_ref, sb_ref)
    neg_inf = [jnp.full((MAX_ROWS, Q_TILE), -jnp.inf, F32)] * 2

    def item(it, e):
        return it * tiles_per_iter + e // npairs, e % npairs, s_bufs[e % 2]

    def score_chunk(tile, pair, s_ref, c, mparts):
        qrows = pl.ds(pl.multiple_of(tile * Q_TILE, Q_TILE), Q_TILE)
        krows = pl.ds(pl.multiple_of(c * KEY_CHUNK, KEY_CHUNK), KEY_CHUNK)
        out = []
        for e in range(2):
            j = 2 * pair + e
            q = q_ref[qrows, q_slots[j] * LANES:(q_slots[j] + 1) * LANES]
            k = k_ref[krows, k_slots[j] * LANES:(k_slots[j] + 1) * LANES]
            st = lax.dot_general(k, q, (((1,), (1,)), ((), ())), preferred_element_type=F32)
            s_ref[e, krows, :] = st
            part = jnp.max(st.reshape(KEY_CHUNK // MAX_ROWS, MAX_ROWS, Q_TILE), axis=0)
            out.append(jnp.maximum(mparts[e], part))
        return out

    def pv_chunk(pair, s_ref, c, m):
        krows = pl.ds(pl.multiple_of(c * KEY_CHUNK, KEY_CHUNK), KEY_CHUNK)
        for e in range(2):
            pt = jnp.exp2((s_ref[e, krows, :] - m[e]).astype(BF16))
            acc_ref[e] += jnp.dot(vt_ref[v_slots[2 * pair + e], c], pt, preferred_element_type=F32)

    def col_max(mparts):
        return [jnp.max(mp, axis=0, keepdims=True) for mp in mparts]

    def finish(tile, pair):
        halves = [acc_ref[e, :MLA_V, :] / acc_ref[e, MLA_V:MLA_V + 1, :] for e in range(2)]
        ot = jnp.concatenate(halves, axis=0)
        rows = pl.ds(pl.multiple_of(tile * Q_TILE, Q_TILE), Q_TILE)
        o_ref[rows, pair * LANES:(pair + 1) * LANES] = ot.T.astype(BF16)
        acc_ref[...] = jnp.zeros_like(acc_ref)

    def item_step(it, e, m_prev):
        tile, pair, s_cur = item(it, e)
        tile_p, pair_p, s_prev = item(it, e - 1) if e > 0 else item(it - 1, items_per_iter - 1)

        def chunk(c, mp):
            mp = score_chunk(tile, pair, s_cur, c, mp)
            pv_chunk(pair_p, s_prev, c, m_prev)
            return mp

        mparts = lax.fori_loop(0, nc, chunk, neg_inf, unroll=CHUNK_UNROLL)
        finish(tile_p, pair_p)
        return col_max(mparts)

    def trip(it, m, first):
        for e in range(first, items_per_iter):
            m = item_step(it, e, m)
        return m

    acc_ref[...] = jnp.zeros_like(acc_ref)
    tile0, pair0, s0 = item(0, 0)

    def first_chunk(c, mp):
        krows = pl.ds(pl.multiple_of(c * KEY_CHUNK, KEY_CHUNK), KEY_CHUNK)
        for slot in sorted(set(v_slots)):
            vt_ref[slot, c] = v_ref[krows, slot * LANES:(slot + 1) * LANES].T[:PV_ROWS, :]
        return score_chunk(tile0, pair0, s0, c, mp)

    m = col_max(lax.fori_loop(0, nc, first_chunk, neg_inf, unroll=CHUNK_UNROLL))
    m = trip(0, m, 1)
    m = lax.fori_loop(1, n_iter, lambda it, m: trip(it, m, 0), m)

    tile_l, pair_l, s_l = item(n_iter - 1, items_per_iter - 1)

    def last_chunk(c, carry):
        pv_chunk(pair_l, s_l, c, m)
        return carry

    lax.fori_loop(0, nc, last_chunk, 0, unroll=CHUNK_UNROLL)
    finish(tile_l, pair_l)


def _attn_call(q, k, v, batch, seq, q_slots, k_slots, v_slots, name):
    nh = len(q_slots)
    qw, kw, vw = ((max(s) + 1) * LANES for s in (q_slots, k_slots, v_slots))
    ow = nh // 2 * LANES
    steps = q.shape[1] // qw
    assert k.shape[1] == steps * kw and v.shape[1] == steps * vw
    return pl.pallas_call(
        functools.partial(_attn_kernel, q_slots=q_slots, k_slots=k_slots, v_slots=v_slots),
        grid=(batch, steps),
        in_specs=[
            pl.BlockSpec((seq, qw), lambda b, p: (b, p)),
            pl.BlockSpec((seq, kw), lambda b, p: (b, p)),
            pl.BlockSpec((seq, vw), lambda b, p: (b, p)),
        ],
        out_specs=pl.BlockSpec((seq, ow), lambda b, p: (b, p)),
        out_shape=jax.ShapeDtypeStruct((q.shape[0], steps * ow), BF16),
        scratch_shapes=[
            pltpu.VMEM((vw // LANES, seq // KEY_CHUNK, PV_ROWS, KEY_CHUNK), BF16),
            pltpu.VMEM((2, seq, Q_TILE), F32),
            pltpu.VMEM((2, seq, Q_TILE), F32),
            pltpu.VMEM((2, PV_ROWS, Q_TILE), F32),
        ],
        compiler_params=pltpu.CompilerParams(vmem_limit_bytes=VMEM_LIMIT_BYTES),
        name=name,
    )(q, k, v)


def _rope_tables(seq):
    rows = seq // GRID_W
    row = jnp.repeat(jnp.arange(rows, dtype=F32), GRID_W)
    col = jnp.tile(jnp.arange(GRID_W, dtype=F32), rows)

    def slot_tables(dim, lead, slot_w):
        half = dim // 2
        inv = ROPE_THETA ** (-jnp.arange(0, half, 2, dtype=F32) / half)
        ar, ac = row[:, None] * inv[None, :], col[:, None] * inv[None, :]
        z = jnp.zeros_like(ar)
        c = jnp.concatenate([jnp.cos(ar)] * 2 + [jnp.cos(ac)] * 2, axis=1)
        sa = jnp.concatenate([-jnp.sin(ar), z, -jnp.sin(ac), z], axis=1)
        sb = jnp.concatenate([z, jnp.sin(ar), z, jnp.sin(ac)], axis=1)
        tail = slot_w - lead - dim
        pad = lambda a, fill: jnp.concatenate(
            [jnp.full((seq, lead), fill, F32), a, jnp.full((seq, tail), fill, F32)], axis=1)
        return pad(c, 1.0), pad(sa, 0.0), pad(sb, 0.0)

    mla = slot_tables(MLA_ROPE, MLA_NOPE, LANES)
    gqa = [jnp.tile(a, (1, LANES // GQA_HD)) for a in slot_tables(GQA_HD, 0, GQA_HD)]
    return jnp.concatenate(list(mla) + gqa, axis=1)


def _layer_params(l, norm_ffn1, w_ffn1_gate, w_ffn1_up, w_ffn1_down, norm_mix, w_in, q_a_norm, w_q_b,
                  kv_a_norm, w_kv_b, gqa_q_norm, gqa_k_norm, w_out, norm_ffn2, w_ffn2_gate, w_ffn2_up,
                  w_ffn2_down):
    row = lambda v: v.reshape(1, -1).astype(F32)
    wi = w_in[l]
    o = np.cumsum([0, Q_LORA, KV_LORA, MLA_ROPE, GQA_HEADS * GQA_HD, GQA_KV_HEADS * GQA_HD, GQA_KV_HEADS * GQA_HD])
    zc = lambda n: jnp.zeros((D_MODEL, n), F32)
    win = jnp.concatenate(
        [wi[:, o[0]:o[1]], wi[:, o[1]:o[2]], zc(MLA_NOPE), wi[:, o[2]:o[3]], zc(LANES - MLA_NOPE - MLA_ROPE),
         wi[:, o[3]:o[4]], wi[:, o[4]:o[5]], wi[:, o[5]:o[6]]], axis=1).astype(BF16)
    assert win.shape[1] == _ZW

    wqb = w_q_b[l].reshape(Q_LORA, MLA_HEADS, MLA_NOPE + MLA_ROPE)
    wqb = jnp.pad(wqb, ((0, 0), (0, 0), (0, LANES - MLA_NOPE - MLA_ROPE))).reshape(Q_LORA, MLA_HEADS * LANES)

    wkv = w_kv_b[l].reshape(KV_LORA, MLA_HEADS, MLA_NOPE + MLA_V)
    wk = jnp.pad(wkv[:, :, :MLA_NOPE], ((0, 0), (0, 0), (0, LANES - MLA_NOPE)))
    wv = jnp.pad(wkv[:, :, MLA_NOPE:], ((0, 0), (0, 0), (0, LANES - MLA_V)))
    wkvb = jnp.concatenate([wk, wv], axis=1).reshape(KV_LORA, 2 * MLA_HEADS * LANES)

    gmat = np.kron(np.eye(GQA_HEADS), np.full((GQA_HD, GQA_HD), 1.0 / GQA_HD))
    return dict(
        g1=row(norm_ffn1[l]), wg1=w_ffn1_gate[l].astype(BF16), wu1=w_ffn1_up[l].astype(BF16),
        wd1=w_ffn1_down[l].astype(BF16),
        gmix=row(norm_mix[l]), win=win, gqa=row(q_a_norm[l]), wqb=wqb.astype(BF16), gkv=row(kv_a_norm[l]),
        wkvb=wkvb.astype(BF16),
        gq=row(jnp.tile(gqa_q_norm[l], GQA_HEADS)) * (GQA_HD ** -0.5 * LOG2E),
        gk=row(jnp.tile(gqa_k_norm[l], GQA_KV_HEADS)),
        gmat=jnp.asarray(gmat, BF16), wout=w_out[l].astype(BF16),
        g2=row(norm_ffn2[l]), wg2=w_ffn2_gate[l].astype(BF16), wu2=w_ffn2_up[l].astype(BF16),
        wd2=w_ffn2_down[l].astype(BF16),
    )


def _trunk(x3, layers, tab, final_g):
    batch, seq, _ = x3.shape
    assert seq % ROW_TILE == 0 and seq % Q_TILE == 0 and seq % GRID_W == 0
    x = x3.reshape(batch * seq, D_MODEL)
    for li, p in enumerate(layers):
        x = _ffn_call(x, p["g1"], p["wg1"], p["wu1"], p["wd1"])
        qm, km, vm, qg, kg, vg = _prep_call(x, tab, seq, p)
        o_mla = _attn_call(qm, km, vm, batch, seq, (0, 1), (0, 1), (0, 1), "attn_mla")
        o_gqa = _attn_call(qg, kg, vg, batch, seq, (0, 0, 1, 1), (0, 1, 0, 1), (0, 0, 0, 0), "attn_gqa")
        x = _ffn_call(x, p["g2"], p["wg2"], p["wu2"], p["wd2"], attn=(o_mla, o_gqa, p["wout"]),
                      final_g=final_g if li == len(layers) - 1 else None)
    return x.reshape(batch, seq, D_MODEL)


def kernel(x_prompt, x_sample, norm_ffn1, w_ffn1_gate, w_ffn1_up, w_ffn1_down, norm_mix, w_in, q_a_norm, w_q_b, kv_a_norm, w_kv_b, gqa_q_norm, gqa_k_norm, w_out, norm_ffn2, w_ffn2_gate, w_ffn2_up, w_ffn2_down, final_norm):
    depth = norm_ffn1.shape[0]
    layers = [
        _layer_params(l, norm_ffn1, w_ffn1_gate, w_ffn1_up, w_ffn1_down, norm_mix, w_in, q_a_norm, w_q_b,
                      kv_a_norm, w_kv_b, gqa_q_norm, gqa_k_norm, w_out, norm_ffn2, w_ffn2_gate, w_ffn2_up,
                      w_ffn2_down)
        for l in range(depth)
    ]
    final_g = final_norm.reshape(1, -1).astype(F32)
    outs = []
    for x3 in (x_prompt, x_sample):
        tab = _rope_tables(x3.shape[1])
        outs.append(_trunk(x3, layers, tab, final_g))
    return tuple(outs)
```

```python
import functools

import jax
import jax.numpy as jnp
import numpy as np
from jax import lax
from jax.experimental import pallas as pl
from jax.experimental.pallas import tpu as pltpu

D_MODEL = 1024
GRID_W = 64
ROPE_THETA = 10000.0
EPS = 1e-6

MLA_HEADS = 8
MLA_NOPE = 64
MLA_ROPE = 32
MLA_V = 64
Q_LORA = 256
KV_LORA = 128
GQA_HEADS = 8
GQA_KV_HEADS = 2
GQA_HD = 64
D_FF = 2816

LANES = 128
LOG2E = 1.4426950408889634

ROW_TILE = 1024
FF_CHUNK = 256
Q_TILE = 512
KEY_CHUNK = 512
PV_ROWS = 128
MAX_ROWS = 32
CHUNK_UNROLL = 4
VMEM_LIMIT_BYTES = 56 * 1024 * 1024

_ZW = Q_LORA + KV_LORA + LANES + GQA_HEADS * GQA_HD + 2 * LANES
_Z_CQ = (0, Q_LORA)
_Z_CKV = (Q_LORA, Q_LORA + KV_LORA)
_Z_KPE = (_Z_CKV[1], _Z_CKV[1] + LANES)
_Z_QG = (_Z_KPE[1], _Z_KPE[1] + GQA_HEADS * GQA_HD)
_Z_KG = (_Z_QG[1], _Z_QG[1] + LANES)
_Z_VG = (_Z_KG[1], _Z_KG[1] + LANES)

BF16 = jnp.bfloat16
F32 = jnp.float32


def _rms(x, g):
    ms = jnp.mean(x * x, axis=-1, keepdims=True)
    return x * lax.rsqrt(ms + EPS) * g


def _rope(x, c, sa, sb, d):
    return x * c + pltpu.roll(x, LANES - d, 1) * sa + pltpu.roll(x, d, 1) * sb


def _ffn_kernel(*refs, has_attn, final):
    it = iter(refs)
    x_ref = next(it)
    if has_attn:
        om_ref, og_ref, wo_ref = next(it), next(it), next(it)
    g_ref, wg_ref, wu_ref, wd_ref = next(it), next(it), next(it), next(it)
    if final:
        fg_ref = next(it)
    o_ref = next(it)
    a_ref = next(it)

    x = x_ref[...]
    if has_attn:
        half = om_ref.shape[1]
        x = x + jnp.dot(om_ref[...], wo_ref[:half, :], preferred_element_type=F32)
        x = x + jnp.dot(og_ref[...], wo_ref[half:, :], preferred_element_type=F32)
    hb = _rms(x, g_ref[...]).astype(BF16)
    for c in range(D_FF // FF_CHUNK):
        sl = slice(c * FF_CHUNK, (c + 1) * FF_CHUNK)
        g = jnp.dot(hb, wg_ref[:, sl], preferred_element_type=F32)
        u = jnp.dot(hb, wu_ref[:, sl], preferred_element_type=F32)
        a_ref[:, sl] = (g * jax.nn.sigmoid(g) * u).astype(BF16)
    y = x + 0.5 * jnp.dot(a_ref[...], wd_ref[...], preferred_element_type=F32)
    if final:
        y = _rms(y, fg_ref[...])
    o_ref[...] = y


def _const_spec(shape):
    return pl.BlockSpec(shape, lambda *_: (0,) * len(shape), pipeline_mode=pl.Buffered(1))


def _ffn_call(x, g, wg, wu, wd, attn=None, final_g=None):
    t = x.shape[0]
    row = lambda w: pl.BlockSpec((ROW_TILE, w), lambda i: (i, 0))
    args, specs = [x], [row(D_MODEL)]
    if attn is not None:
        o_mla, o_gqa, w_out = attn
        args += [o_mla, o_gqa, w_out]
        specs += [row(o_mla.shape[1]), row(o_gqa.shape[1]), _const_spec(w_out.shape)]
    args += [g, wg, wu, wd]
    specs += [_const_spec(a.shape) for a in (g, wg, wu, wd)]
    if final_g is not None:
        args.append(final_g)
        specs.append(_const_spec(final_g.shape))
    return pl.pallas_call(
        functools.partial(_ffn_kernel, has_attn=attn is not None, final=final_g is not None),
        grid=(t // ROW_TILE,),
        in_specs=specs,
        out_specs=row(D_MODEL),
        out_shape=jax.ShapeDtypeStruct((t, D_MODEL), F32),
        scratch_shapes=[pltpu.VMEM((ROW_TILE, D_FF), BF16)],
        compiler_params=pltpu.CompilerParams(vmem_limit_bytes=VMEM_LIMIT_BYTES),
        name="ffn_attn" if attn is not None else "ffn",
    )(*args)


def _prep_kernel(x_ref, tab_ref, gmix_ref, win_ref, gqa_ref, wqb_ref, gkv_ref, wkvb_ref, gq_ref, gk_ref,
                 gmat_ref, qm_ref, km_ref, vm_ref, qg_ref, kg_ref, vg_ref):
    hb = _rms(x_ref[...], gmix_ref[...]).astype(BF16)
    z = jnp.dot(hb, win_ref[...], preferred_element_type=F32)

    mc, msa, msb = (tab_ref[:, j * LANES:(j + 1) * LANES] for j in range(3))
    gc, gsa, gsb = (tab_ref[:, j * LANES:(j + 1) * LANES] for j in range(3, 6))
    lane = lax.broadcasted_iota(jnp.int32, (1, LANES), 1)
    lo = lane < GQA_HD
    one = (lane == MLA_V).astype(F32)

    cq = _rms(z[:, _Z_CQ[0]:_Z_CQ[1]], gqa_ref[...]).astype(BF16)
    qa = jnp.dot(cq, wqb_ref[...], preferred_element_type=F32)
    q_scale = (MLA_NOPE + MLA_ROPE) ** -0.5 * LOG2E
    for h in range(MLA_HEADS):
        sl = slice(h * LANES, (h + 1) * LANES)
        qm_ref[:, sl] = (_rope(qa[:, sl], mc, msa, msb, MLA_ROPE // 4) * q_scale).astype(BF16)

    ckv = _rms(z[:, _Z_CKV[0]:_Z_CKV[1]], gkv_ref[...]).astype(BF16)
    kv = jnp.dot(ckv, wkvb_ref[...], preferred_element_type=F32)
    kpe = _rope(z[:, _Z_KPE[0]:_Z_KPE[1]], mc, msa, msb, MLA_ROPE // 4)
    for h in range(MLA_HEADS):
        sl = slice(h * LANES, (h + 1) * LANES)
        km_ref[:, sl] = (kv[:, sl] + kpe).astype(BF16)
        vsl = slice((MLA_HEADS + h) * LANES, (MLA_HEADS + h + 1) * LANES)
        vm_ref[:, sl] = (kv[:, vsl] + one).astype(BF16)

    def head_mean_sq(v, gmat):
        sq = v * v
        hi = sq.astype(BF16)
        lo_part = (sq - hi.astype(F32)).astype(BF16)
        return (jnp.dot(hi, gmat, preferred_element_type=F32)
                + jnp.dot(lo_part, gmat, preferred_element_type=F32))

    qg = z[:, _Z_QG[0]:_Z_QG[1]]
    qg = qg * lax.rsqrt(head_mean_sq(qg, gmat_ref[...]) + EPS) * gq_ref[...]
    for j in range(GQA_HEADS // 2):
        sl = slice(j * LANES, (j + 1) * LANES)
        qg_ref[:, sl] = _rope(qg[:, sl], gc, gsa, gsb, GQA_HD // 4).astype(BF16)

    kg = z[:, _Z_KG[0]:_Z_KG[1]]
    kg = kg * lax.rsqrt(head_mean_sq(kg, gmat_ref[:LANES, :LANES]) + EPS) * gk_ref[...]
    kg = _rope(kg, gc, gsa, gsb, GQA_HD // 4)
    kg_sw = pltpu.roll(kg, GQA_HD, 1)
    zero = jnp.zeros_like(kg)
    kg_ref[:, 0 * LANES:1 * LANES] = jnp.where(lo, kg, zero).astype(BF16)
    kg_ref[:, 1 * LANES:2 * LANES] = jnp.where(lo, zero, kg_sw).astype(BF16)
    kg_ref[:, 2 * LANES:3 * LANES] = jnp.where(lo, kg_sw, zero).astype(BF16)
    kg_ref[:, 3 * LANES:4 * LANES] = jnp.where(lo, zero, kg).astype(BF16)

    vg = z[:, _Z_VG[0]:_Z_VG[1]]
    vg_sw = pltpu.roll(vg, GQA_HD, 1)
    vg_ref[:, 0 * LANES:1 * LANES] = jnp.where(lo, vg, one).astype(BF16)
    vg_ref[:, 1 * LANES:2 * LANES] = jnp.where(lo, vg_sw, one).astype(BF16)


def _prep_call(x, tab, seq, p):
    t = x.shape[0]
    tiles_per_seq = seq // ROW_TILE
    row = lambda w: pl.BlockSpec((ROW_TILE, w), lambda i: (i, 0))
    consts = [p["gmix"], p["win"], p["gqa"], p["wqb"], p["gkv"], p["wkvb"], p["gq"], p["gk"], p["gmat"]]
    widths = (MLA_HEADS * LANES,) * 3 + (GQA_HEADS // 2 * LANES, 2 * GQA_KV_HEADS * LANES, GQA_KV_HEADS * LANES)
    return pl.pallas_call(
        _prep_kernel,
        grid=(t // ROW_TILE,),
        in_specs=[row(D_MODEL), pl.BlockSpec((ROW_TILE, tab.shape[1]), lambda i: (i % tiles_per_seq, 0))]
        + [_const_spec(c.shape) for c in consts],
        out_specs=[row(w) for w in widths],
        out_shape=[jax.ShapeDtypeStruct((t, w), BF16) for w in widths],
        compiler_params=pltpu.CompilerParams(vmem_limit_bytes=VMEM_LIMIT_BYTES),
        name="prep",
    )(x, tab, *consts)


def _attn_kernel(q_ref, k_ref, v_ref, o_ref, vt_ref, sa_ref, sb_ref, acc_ref, *, q_slots, k_slots, v_slots):
    seq = k_ref.shape[0]
    nq, nc, npairs = seq // Q_TILE, seq // KEY_CHUNK, len(q_slots) // 2
    tiles_per_iter = 2 if npairs % 2 else 1
    items_per_iter = npairs * tiles_per_iter
    n_iter = nq // tiles_per_iter
    assert len(q_slots) % 2 == 0 and nq % tiles_per_iter == 0
    s_bufs = (sa_ref, sb_ref)
    neg_inf = [jnp.full((MAX_ROWS, Q_TILE), -jnp.inf, F32)] * 2

    def item(it, e):
        return it * tiles_per_iter + e // npairs, e % npairs, s_bufs[e % 2]

    def score_chunk(tile, pair, s_ref, c, mparts, heads=(0, 1)):
        qrows = pl.ds(pl.multiple_of(tile * Q_TILE, Q_TILE), Q_TILE)
        krows = pl.ds(pl.multiple_of(c * KEY_CHUNK, KEY_CHUNK), KEY_CHUNK)
        out = []
        for e in heads:
            j = 2 * pair + e
            q = q_ref[qrows, q_slots[j] * LANES:(q_slots[j] + 1) * LANES]
            k = k_ref[krows, k_slots[j] * LANES:(k_slots[j] + 1) * LANES]
            st = lax.dot_general(k, q, (((1,), (1,)), ((), ())), preferred_element_type=F32)
            s_ref[e, krows, :] = st
            part = jnp.max(st.reshape(KEY_CHUNK // MAX_ROWS, MAX_ROWS, Q_TILE), axis=0)
            out.append(jnp.maximum(mparts[e], part))
        return out

    def pv_chunk(pair, s_ref, c, m, heads=(0, 1)):
        krows = pl.ds(pl.multiple_of(c * KEY_CHUNK, KEY_CHUNK), KEY_CHUNK)
        for e in heads:
            pt = jnp.exp2((s_ref[e, krows, :] - m[e]).astype(BF16))
            acc_ref[e] += jnp.dot(vt_ref[v_slots[2 * pair + e], c], pt, preferred_element_type=F32)

    def col_max(mparts):
        return [jnp.max(mp, axis=0, keepdims=True) for mp in mparts]

    def finish(tile, pair):
        halves = [acc_ref[e, :MLA_V, :] / acc_ref[e, MLA_V:MLA_V + 1, :] for e in range(2)]
        ot = jnp.concatenate(halves, axis=0)
        rows = pl.ds(pl.multiple_of(tile * Q_TILE, Q_TILE), Q_TILE)
        o_ref[rows, pair * LANES:(pair + 1) * LANES] = ot.T.astype(BF16)
        acc_ref[...] = jnp.zeros_like(acc_ref)

    def item_step(it, e, m_prev):
        tile, pair, s_cur = item(it, e)
        tile_p, pair_p, s_prev = item(it, e - 1) if e > 0 else item(it - 1, items_per_iter - 1)

        def chunk(c, mp):
            out = []
            for e in range(2):
                out.append(score_chunk(tile, pair, s_cur, c, mp, heads=(e,))[0])
                pv_chunk(pair_p, s_prev, c, m_prev, heads=(e,))
            return out

        mparts = lax.fori_loop(0, nc, chunk, neg_inf, unroll=CHUNK_UNROLL)
        finish(tile_p, pair_p)
        return col_max(mparts)

    def trip(it, m, first):
        for e in range(first, items_per_iter):
            m = item_step(it, e, m)
        return m

    acc_ref[...] = jnp.zeros_like(acc_ref)
    tile0, pair0, s0 = item(0, 0)

    def first_chunk(c, mp):
        krows = pl.ds(pl.multiple_of(c * KEY_CHUNK, KEY_CHUNK), KEY_CHUNK)
        for slot in sorted(set(v_slots)):
            vt_ref[slot, c] = v_ref[krows, slot * LANES:(slot + 1) * LANES].T[:PV_ROWS, :]
        return score_chunk(tile0, pair0, s0, c, mp)

    m = col_max(lax.fori_loop(0, nc, first_chunk, neg_inf, unroll=CHUNK_UNROLL))
    m = trip(0, m, 1)
    m = lax.fori_loop(1, n_iter, lambda it, m: trip(it, m, 0), m)

    tile_l, pair_l, s_l = item(n_iter - 1, items_per_iter - 1)

    def last_chunk(c, carry):
        pv_chunk(pair_l, s_l, c, m)
        return carry

    lax.fori_loop(0, nc, last_chunk, 0, unroll=CHUNK_UNROLL)
    finish(tile_l, pair_l)


def _attn_call(q, k, v, batch, seq, q_slots, k_slots, v_slots, name):
    nh = len(q_slots)
    qw, kw, vw = ((max(s) + 1) * LANES for s in (q_slots, k_slots, v_slots))
    ow = nh // 2 * LANES
    steps = q.shape[1] // qw
    assert k.shape[1] == steps * kw and v.shape[1] == steps * vw
    return pl.pallas_call(
        functools.partial(_attn_kernel, q_slots=q_slots, k_slots=k_slots, v_slots=v_slots),
        grid=(batch, steps),
        in_specs=[
            pl.BlockSpec((seq, qw), lambda b, p: (b, p)),
            pl.BlockSpec((seq, kw), lambda b, p: (b, p)),
            pl.BlockSpec((seq, vw), lambda b, p: (b, p)),
        ],
        out_specs=pl.BlockSpec((seq, ow), lambda b, p: (b, p)),
        out_shape=jax.ShapeDtypeStruct((q.shape[0], steps * ow), BF16),
        scratch_shapes=[
            pltpu.VMEM((vw // LANES, seq // KEY_CHUNK, PV_ROWS, KEY_CHUNK), BF16),
            pltpu.VMEM((2, seq, Q_TILE), F32),
            pltpu.VMEM((2, seq, Q_TILE), F32),
            pltpu.VMEM((2, PV_ROWS, Q_TILE), F32),
        ],
        compiler_params=pltpu.CompilerParams(vmem_limit_bytes=VMEM_LIMIT_BYTES),
        name=name,
    )(q, k, v)


def _rope_tables(seq):
    rows = seq // GRID_W
    row = jnp.repeat(jnp.arange(rows, dtype=F32), GRID_W)
    col = jnp.tile(jnp.arange(GRID_W, dtype=F32), rows)

    def slot_tables(dim, lead, slot_w):
        half = dim // 2
        inv = ROPE_THETA ** (-jnp.arange(0, half, 2, dtype=F32) / half)
        ar, ac = row[:, None] * inv[None, :], col[:, None] * inv[None, :]
        z = jnp.zeros_like(ar)
        c = jnp.concatenate([jnp.cos(ar)] * 2 + [jnp.cos(ac)] * 2, axis=1)
        sa = jnp.concatenate([-jnp.sin(ar), z, -jnp.sin(ac), z], axis=1)
        sb = jnp.concatenate([z, jnp.sin(ar), z, jnp.sin(ac)], axis=1)
        tail = slot_w - lead - dim
        pad = lambda a, fill: jnp.concatenate(
            [jnp.full((seq, lead), fill, F32), a, jnp.full((seq, tail), fill, F32)], axis=1)
        return pad(c, 1.0), pad(sa, 0.0), pad(sb, 0.0)

    mla = slot_tables(MLA_ROPE, MLA_NOPE, LANES)
    gqa = [jnp.tile(a, (1, LANES // GQA_HD)) for a in slot_tables(GQA_HD, 0, GQA_HD)]
    return jnp.concatenate(list(mla) + gqa, axis=1)


def _layer_params(l, norm_ffn1, w_ffn1_gate, w_ffn1_up, w_ffn1_down, norm_mix, w_in, q_a_norm, w_q_b,
                  kv_a_norm, w_kv_b, gqa_q_norm, gqa_k_norm, w_out, norm_ffn2, w_ffn2_gate, w_ffn2_up,
                  w_ffn2_down):
    row = lambda v: v.reshape(1, -1).astype(F32)
    wi = w_in[l]
    o = np.cumsum([0, Q_LORA, KV_LORA, MLA_ROPE, GQA_HEADS * GQA_HD, GQA_KV_HEADS * GQA_HD, GQA_KV_HEADS * GQA_HD])
    zc = lambda n: jnp.zeros((D_MODEL, n), F32)
    win = jnp.concatenate(
        [wi[:, o[0]:o[1]], wi[:, o[1]:o[2]], zc(MLA_NOPE), wi[:, o[2]:o[3]], zc(LANES - MLA_NOPE - MLA_ROPE),
         wi[:, o[3]:o[4]], wi[:, o[4]:o[5]], wi[:, o[5]:o[6]]], axis=1).astype(BF16)
    assert win.shape[1] == _ZW

    wqb = w_q_b[l].reshape(Q_LORA, MLA_HEADS, MLA_NOPE + MLA_ROPE)
    wqb = jnp.pad(wqb, ((0, 0), (0, 0), (0, LANES - MLA_NOPE - MLA_ROPE))).reshape(Q_LORA, MLA_HEADS * LANES)

    wkv = w_kv_b[l].reshape(KV_LORA, MLA_HEADS, MLA_NOPE + MLA_V)
    wk = jnp.pad(wkv[:, :, :MLA_NOPE], ((0, 0), (0, 0), (0, LANES - MLA_NOPE)))
    wv = jnp.pad(wkv[:, :, MLA_NOPE:], ((0, 0), (0, 0), (0, LANES - MLA_V)))
    wkvb = jnp.concatenate([wk, wv], axis=1).reshape(KV_LORA, 2 * MLA_HEADS * LANES)

    gmat = np.kron(np.eye(GQA_HEADS), np.full((GQA_HD, GQA_HD), 1.0 / GQA_HD))
    return dict(
        g1=row(norm_ffn1[l]), wg1=w_ffn1_gate[l].astype(BF16), wu1=w_ffn1_up[l].astype(BF16),
        wd1=w_ffn1_down[l].astype(BF16),
        gmix=row(norm_mix[l]), win=win, gqa=row(q_a_norm[l]), wqb=wqb.astype(BF16), gkv=row(kv_a_norm[l]),
        wkvb=wkvb.astype(BF16),
        gq=row(jnp.tile(gqa_q_norm[l], GQA_HEADS)) * (GQA_HD ** -0.5 * LOG2E),
        gk=row(jnp.tile(gqa_k_norm[l], GQA_KV_HEADS)),
        gmat=jnp.asarray(gmat, BF16), wout=w_out[l].astype(BF16),
        g2=row(norm_ffn2[l]), wg2=w_ffn2_gate[l].astype(BF16), wu2=w_ffn2_up[l].astype(BF16),
        wd2=w_ffn2_down[l].astype(BF16),
    )


def _trunk(x3, layers, tab, final_g):
    batch, seq, _ = x3.shape
    assert seq % ROW_TILE == 0 and seq % Q_TILE == 0 and seq % GRID_W == 0
    x = x3.reshape(batch * seq, D_MODEL)
    for li, p in enumerate(layers):
        x = _ffn_call(x, p["g1"], p["wg1"], p["wu1"], p["wd1"])
        qm, km, vm, qg, kg, vg = _prep_call(x, tab, seq, p)
        o_mla = _attn_call(qm, km, vm, batch, seq, (0, 1), (0, 1), (0, 1), "attn_mla")
        o_gqa = _attn_call(qg, kg, vg, batch, seq, (0, 0, 1, 1), (0, 1, 0, 1), (0, 0, 0, 0), "attn_gqa")
        x = _ffn_call(x, p["g2"], p["wg2"], p["wu2"], p["wd2"], attn=(o_mla, o_gqa, p["wout"]),
                      final_g=final_g if li == len(layers) - 1 else None)
    return x.reshape(batch, seq, D_MODEL)


def kernel(x_prompt, x_sample, norm_ffn1, w_ffn1_gate, w_ffn1_up, w_ffn1_down, norm_mix, w_in, q_a_norm, w_q_b, kv_a_norm, w_kv_b, gqa_q_norm, gqa_k_norm, w_out, norm_ffn2, w_ffn2_gate, w_ffn2_up, w_ffn2_down, final_norm):
    depth = norm_ffn1.shape[0]
    layers = [
        _layer_params(l, norm_ffn1, w_ffn1_gate, w_ffn1_up, w_ffn1_down, norm_mix, w_in, q_a_norm, w_q_b,
                      kv_a_norm, w_kv_b, gqa_q_norm, gqa_k_norm, w_out, norm_ffn2, w_ffn2_gate, w_ffn2_up,
                      w_ffn2_down)
        for l in range(depth)
    ]
    final_g = final_norm.reshape(1, -1).astype(F32)
    outs = []
    for x3 in (x_prompt, x_sample):
        tab = _rope_tables(x3.shape[1])
        outs.append(_trunk(x3, layers, tab, final_g))
    return tuple(outs)
```

```python
import functools

import jax
import jax.numpy as jnp
import numpy as np
from jax import lax
from jax.experimental import pallas as pl
from jax.experimental.pallas import tpu as pltpu

D_MODEL = 1024
GRID_W = 64
ROPE_THETA = 10000.0
EPS = 1e-6

MLA_HEADS = 8
MLA_NOPE = 64
MLA_ROPE = 32
MLA_V = 64
Q_LORA = 256
KV_LORA = 128
GQA_HEADS = 8
GQA_KV_HEADS = 2
GQA_HD = 64
D_FF = 2816

LANES = 128
LOG2E = 1.4426950408889634

ROW_TILE = 1024
FF_CHUNK = 256
Q_TILE = 512
KEY_CHUNK = 512
PV_ROWS = 128
MAX_ROWS = 16
CHUNK_UNROLL = 4
VMEM_LIMIT_BYTES = 56 * 1024 * 1024

_ZW = Q_LORA + KV_LORA + LANES + GQA_HEADS * GQA_HD + 2 * LANES
_Z_CQ = (0, Q_LORA)
_Z_CKV = (Q_LORA, Q_LORA + KV_LORA)
_Z_KPE = (_Z_CKV[1], _Z_CKV[1] + LANES)
_Z_QG = (_Z_KPE[1], _Z_KPE[1] + GQA_HEADS * GQA_HD)
_Z_KG = (_Z_QG[1], _Z_QG[1] + LANES)
_Z_VG = (_Z_KG[1], _Z_KG[1] + LANES)

BF16 = jnp.bfloat16
F32 = jnp.float32


def _rms(x, g):
    ms = jnp.mean(x * x, axis=-1, keepdims=True)
    return x * lax.rsqrt(ms + EPS) * g


def _rope(x, c, sa, sb, d):
    return x * c + pltpu.roll(x, LANES - d, 1) * sa + pltpu.roll(x, d, 1) * sb


def _ffn_kernel(*refs, has_attn, final):
    it = iter(refs)
    x_ref = next(it)
    if has_attn:
        om_ref, og_ref, wo_ref = next(it), next(it), next(it)
    g_ref, wg_ref, wu_ref, wd_ref = next(it), next(it), next(it), next(it)
    if final:
        fg_ref = next(it)
    o_ref = next(it)
    a_ref = next(it)

    x = x_ref[...]
    if has_attn:
        half = om_ref.shape[1]
        x = x + jnp.dot(om_ref[...], wo_ref[:half, :], preferred_element_type=F32)
        x = x + jnp.dot(og_ref[...], wo_ref[half:, :], preferred_element_type=F32)
    hb = _rms(x, g_ref[...]).astype(BF16)
    for c in range(D_FF // FF_CHUNK):
        sl = slice(c * FF_CHUNK, (c + 1) * FF_CHUNK)
        g = jnp.dot(hb, wg_ref[:, sl], preferred_element_type=F32)
        u = jnp.dot(hb, wu_ref[:, sl], preferred_element_type=F32)
        a_ref[:, sl] = (g * jax.nn.sigmoid(g) * u).astype(BF16)
    y = x + 0.5 * jnp.dot(a_ref[...], wd_ref[...], preferred_element_type=F32)
    if final:
        y = _rms(y, fg_ref[...])
    o_ref[...] = y


def _const_spec(shape):
    return pl.BlockSpec(shape, lambda *_: (0,) * len(shape), pipeline_mode=pl.Buffered(1))


def _ffn_call(x, g, wg, wu, wd, attn=None, final_g=None):
    t = x.shape[0]
    row = lambda w: pl.BlockSpec((ROW_TILE, w), lambda i: (i, 0))
    args, specs = [x], [row(D_MODEL)]
    if attn is not None:
        o_mla, o_gqa, w_out = attn
        args += [o_mla, o_gqa, w_out]
        specs += [row(o_mla.shape[1]), row(o_gqa.shape[1]), _const_spec(w_out.shape)]
    args += [g, wg, wu, wd]
    specs += [_const_spec(a.shape) for a in (g, wg, wu, wd)]
    if final_g is not None:
        args.append(final_g)
        specs.append(_const_spec(final_g.shape))
    return pl.pallas_call(
        functools.partial(_ffn_kernel, has_attn=attn is not None, final=final_g is not None),
        grid=(t // ROW_TILE,),
        in_specs=specs,
        out_specs=row(D_MODEL),
        out_shape=jax.ShapeDtypeStruct((t, D_MODEL), F32),
        scratch_shapes=[pltpu.VMEM((ROW_TILE, D_FF), BF16)],
        compiler_params=pltpu.CompilerParams(vmem_limit_bytes=VMEM_LIMIT_BYTES),
        name="ffn_attn" if attn is not None else "ffn",
    )(*args)


def _prep_kernel(x_ref, tab_ref, gmix_ref, win_ref, gqa_ref, wqb_ref, gkv_ref, wkvb_ref, gq_ref, gk_ref,
                 gmat_ref, qm_ref, km_ref, vm_ref, qg_ref, kg_ref, vg_ref):
    hb = _rms(x_ref[...], gmix_ref[...]).astype(BF16)
    z = jnp.dot(hb, win_ref[...], preferred_element_type=F32)

    mc, msa, msb = (tab_ref[:, j * LANES:(j + 1) * LANES] for j in range(3))
    gc, gsa, gsb = (tab_ref[:, j * LANES:(j + 1) * LANES] for j in range(3, 6))
    lane = lax.broadcasted_iota(jnp.int32, (1, LANES), 1)
    lo = lane < GQA_HD
    one = (lane == MLA_V).astype(F32)

    cq = _rms(z[:, _Z_CQ[0]:_Z_CQ[1]], gqa_ref[...]).astype(BF16)
    qa = jnp.dot(cq, wqb_ref[...], preferred_element_type=F32)
    q_scale = (MLA_NOPE + MLA_ROPE) ** -0.5 * LOG2E
    for h in range(MLA_HEADS):
        sl = slice(h * LANES, (h + 1) * LANES)
        qm_ref[:, sl] = (_rope(qa[:, sl], mc, msa, msb, MLA_ROPE // 4) * q_scale).astype(BF16)

    ckv = _rms(z[:, _Z_CKV[0]:_Z_CKV[1]], gkv_ref[...]).astype(BF16)
    kv = jnp.dot(ckv, wkvb_ref[...], preferred_element_type=F32)
    kpe = _rope(z[:, _Z_KPE[0]:_Z_KPE[1]], mc, msa, msb, MLA_ROPE // 4)
    for h in range(MLA_HEADS):
        sl = slice(h * LANES, (h + 1) * LANES)
        km_ref[:, sl] = (kv[:, sl] + kpe).astype(BF16)
        vsl = slice((MLA_HEADS + h) * LANES, (MLA_HEADS + h + 1) * LANES)
        vm_ref[:, sl] = (kv[:, vsl] + one).astype(BF16)

    def head_mean_sq(v, gmat):
        sq = v * v
        hi = sq.astype(BF16)
        lo_part = (sq - hi.astype(F32)).astype(BF16)
        return (jnp.dot(hi, gmat, preferred_element_type=F32)
                + jnp.dot(lo_part, gmat, preferred_element_type=F32))

    qg = z[:, _Z_QG[0]:_Z_QG[1]]
    qg = qg * lax.rsqrt(head_mean_sq(qg, gmat_ref[...]) + EPS) * gq_ref[...]
    for j in range(GQA_HEADS // 2):
        sl = slice(j * LANES, (j + 1) * LANES)
        qg_ref[:, sl] = _rope(qg[:, sl], gc, gsa, gsb, GQA_HD // 4).astype(BF16)

    kg = z[:, _Z_KG[0]:_Z_KG[1]]
    kg = kg * lax.rsqrt(head_mean_sq(kg, gmat_ref[:LANES, :LANES]) + EPS) * gk_ref[...]
    kg = _rope(kg, gc, gsa, gsb, GQA_HD // 4)
    kg_sw = pltpu.roll(kg, GQA_HD, 1)
    zero = jnp.zeros_like(kg)
    kg_ref[:, 0 * LANES:1 * LANES] = jnp.where(lo, kg, zero).astype(BF16)
    kg_ref[:, 1 * LANES:2 * LANES] = jnp.where(lo, zero, kg_sw).astype(BF16)
    kg_ref[:, 2 * LANES:3 * LANES] = jnp.where(lo, kg_sw, zero).astype(BF16)
    kg_ref[:, 3 * LANES:4 * LANES] = jnp.where(lo, zero, kg).astype(BF16)

    vg = z[:, _Z_VG[0]:_Z_VG[1]]
    vg_sw = pltpu.roll(vg, GQA_HD, 1)
    vg_ref[:, 0 * LANES:1 * LANES] = jnp.where(lo, vg, one).astype(BF16)
    vg_ref[:, 1 * LANES:2 * LANES] = jnp.where(lo, vg_sw, one).astype(BF16)


def _prep_call(x, tab, seq, p):
    t = x.shape[0]
    tiles_per_seq = seq // ROW_TILE
    row = lambda w: pl.BlockSpec((ROW_TILE, w), lambda i: (i, 0))
    consts = [p["gmix"], p["win"], p["gqa"], p["wqb"], p["gkv"], p["wkvb"], p["gq"], p["gk"], p["gmat"]]
    widths = (MLA_HEADS * LANES,) * 3 + (GQA_HEADS // 2 * LANES, 2 * GQA_KV_HEADS * LANES, GQA_KV_HEADS * LANES)
    return pl.pallas_call(
        _prep_kernel,
        grid=(t // ROW_TILE,),
        in_specs=[row(D_MODEL), pl.BlockSpec((ROW_TILE, tab.shape[1]), lambda i: (i % tiles_per_seq, 0))]
        + [_const_spec(c.shape) for c in consts],
        out_specs=[row(w) for w in widths],
        out_shape=[jax.ShapeDtypeStruct((t, w), BF16) for w in widths],
        compiler_params=pltpu.CompilerParams(vmem_limit_bytes=VMEM_LIMIT_BYTES),
        name="prep",
    )(x, tab, *consts)


def _attn_kernel(q_ref, k_ref, v_ref, o_ref, vt_ref, sa_ref, sb_ref, acc_ref, *, q_slots, k_slots, v_slots):
    seq = k_ref.shape[0]
    nq, nc, npairs = seq // Q_TILE, seq // KEY_CHUNK, len(q_slots) // 2
    tiles_per_iter = 2 if npairs % 2 else 1
    items_per_iter = npairs * tiles_per_iter
    n_iter = nq // tiles_per_iter
    assert len(q_slots) % 2 == 0 and nq % tiles_per_iter == 0
    s_bufs = (sa_ref, sb_ref)
    neg_inf = [jnp.full((MAX_ROWS, Q_TILE), -jnp.inf, F32)] * 2

    def item(it, e):
        return it * tiles_per_iter + e // npairs, e % npairs, s_bufs[e % 2]

    def score_chunk(tile, pair, s_ref, c, mparts, heads=(0, 1)):
        qrows = pl.ds(pl.multiple_of(tile * Q_TILE, Q_TILE), Q_TILE)
        krows = pl.ds(pl.multiple_of(c * KEY_CHUNK, KEY_CHUNK), KEY_CHUNK)
        out = []
        for e in heads:
            j = 2 * pair + e
            q = q_ref[qrows, q_slots[j] * LANES:(q_slots[j] + 1) * LANES]
            k = k_ref[krows, k_slots[j] * LANES:(k_slots[j] + 1) * LANES]
            st = lax.dot_general(k, q, (((1,), (1,)), ((), ())), preferred_element_type=F32)
            s_ref[e, krows, :] = st
            part = jnp.max(st.reshape(KEY_CHUNK // MAX_ROWS, MAX_ROWS, Q_TILE), axis=0)
            out.append(jnp.maximum(mparts[e], part))
        return out

    def pv_chunk(pair, s_ref, c, m, heads=(0, 1)):
        krows = pl.ds(pl.multiple_of(c * KEY_CHUNK, KEY_CHUNK), KEY_CHUNK)
        for e in heads:
            pt = jnp.exp2((s_ref[e, krows, :] - m[e]).astype(BF16))
            acc_ref[e] += jnp.dot(vt_ref[v_slots[2 * pair + e], c], pt, preferred_element_type=F32)

    def col_max(mparts):
        return [jnp.max(mp, axis=0, keepdims=True) for mp in mparts]

    def finish(tile, pair):
        halves = [acc_ref[e, :MLA_V, :] / acc_ref[e, MLA_V:MLA_V + 1, :] for e in range(2)]
        ot = jnp.concatenate(halves, axis=0)
        rows = pl.ds(pl.multiple_of(tile * Q_TILE, Q_TILE), Q_TILE)
        o_ref[rows, pair * LANES:(pair + 1) * LANES] = ot.T.astype(BF16)
        acc_ref[...] = jnp.zeros_like(acc_ref)

    def item_step(it, e, m_prev):
        tile, pair, s_cur = item(it, e)
        tile_p, pair_p, s_prev = item(it, e - 1) if e > 0 else item(it - 1, items_per_iter - 1)

        def chunk(c, mp):
            out = []
            for e in range(2):
                out.append(score_chunk(tile, pair, s_cur, c, mp, heads=(e,))[0])
                pv_chunk(pair_p, s_prev, c, m_prev, heads=(e,))
            return out

        mparts = lax.fori_loop(0, nc, chunk, neg_inf, unroll=CHUNK_UNROLL)
        finish(tile_p, pair_p)
        return col_max(mparts)

    def trip(it, m, first):
        for e in range(first, items_per_iter):
            m = item_step(it, e, m)
        return m

    acc_ref[...] = jnp.zeros_like(acc_ref)
    tile0, pair0, s0 = item(0, 0)

    def first_chunk(c, mp):
        krows = pl.ds(pl.multiple_of(c * KEY_CHUNK, KEY_CHUNK), KEY_CHUNK)
        for slot in sorted(set(v_slots)):
            vt_ref[slot, c] = v_ref[krows, slot * LANES:(slot + 1) * LANES].T[:PV_ROWS, :]
        return score_chunk(tile0, pair0, s0, c, mp)

    m = col_max(lax.fori_loop(0, nc, first_chunk, neg_inf, unroll=CHUNK_UNROLL))
    m = trip(0, m, 1)
    m = lax.fori_loop(1, n_iter, lambda it, m: trip(it, m, 0), m)

    tile_l, pair_l, s_l = item(n_iter - 1, items_per_iter - 1)

    def last_chunk(c, carry):
        pv_chunk(pair_l, s_l, c, m)
        return carry

    lax.fori_loop(0, nc, last_chunk, 0, unroll=CHUNK_UNROLL)
    finish(tile_l, pair_l)


def _attn_call(q, k, v, batch, seq, q_slots, k_slots, v_slots, name):
    nh = len(q_slots)
    qw, kw, vw = ((max(s) + 1) * LANES for s in (q_slots, k_slots, v_slots))
    ow = nh // 2 * LANES
    steps = q.shape[1] // qw
    assert k.shape[1] == steps * kw and v.shape[1] == steps * vw
    return pl.pallas_call(
        functools.partial(_attn_kernel, q_slots=q_slots, k_slots=k_slots, v_slots=v_slots),
        grid=(batch, steps),
        in_specs=[
            pl.BlockSpec((seq, qw), lambda b, p: (b, p)),
            pl.BlockSpec((seq, kw), lambda b, p: (b, p)),
            pl.BlockSpec((seq, vw), lambda b, p: (b, p)),
        ],
        out_specs=pl.BlockSpec((seq, ow), lambda b, p: (b, p)),
        out_shape=jax.ShapeDtypeStruct((q.shape[0], steps * ow), BF16),
        scratch_shapes=[
            pltpu.VMEM((vw // LANES, seq // KEY_CHUNK, PV_ROWS, KEY_CHUNK), BF16),
            pltpu.VMEM((2, seq, Q_TILE), F32),
            pltpu.VMEM((2, seq, Q_TILE), F32),
            pltpu.VMEM((2, PV_ROWS, Q_TILE), F32),
        ],
        compiler_params=pltpu.CompilerParams(vmem_limit_bytes=VMEM_LIMIT_BYTES),
        name=name,
    )(q, k, v)


def _rope_tables(seq):
    rows = seq // GRID_W
    row = jnp.repeat(jnp.arange(rows, dtype=F32), GRID_W)
    col = jnp.tile(jnp.arange(GRID_W, dtype=F32), rows)

    def slot_tables(dim, lead, slot_w):
        half = dim // 2
        inv = ROPE_THETA ** (-jnp.arange(0, half, 2, dtype=F32) / half)
        ar, ac = row[:, None] * inv[None, :], col[:, None] * inv[None, :]
        z = jnp.zeros_like(ar)
        c = jnp.concatenate([jnp.cos(ar)] * 2 + [jnp.cos(ac)] * 2, axis=1)
        sa = jnp.concatenate([-jnp.sin(ar), z, -jnp.sin(ac), z], axis=1)
        sb = jnp.concatenate([z, jnp.sin(ar), z, jnp.sin(ac)], axis=1)
        tail = slot_w - lead - dim
        pad = lambda a, fill: jnp.concatenate(
            [jnp.full((seq, lead), fill, F32), a, jnp.full((seq, tail), fill, F32)], axis=1)
        return pad(c, 1.0), pad(sa, 0.0), pad(sb, 0.0)

    mla = slot_tables(MLA_ROPE, MLA_NOPE, LANES)
    gqa = [jnp.tile(a, (1, LANES // GQA_HD)) for a in slot_tables(GQA_HD, 0, GQA_HD)]
    return jnp.concatenate(list(mla) + gqa, axis=1)


def _layer_params(l, norm_ffn1, w_ffn1_gate, w_ffn1_up, w_ffn1_down, norm_mix, w_in, q_a_norm, w_q_b,
                  kv_a_norm, w_kv_b, gqa_q_norm, gqa_k_norm, w_out, norm_ffn2, w_ffn2_gate, w_ffn2_up,
                  w_ffn2_down):
    row = lambda v: v.reshape(1, -1).astype(F32)
    wi = w_in[l]
    o = np.cumsum([0, Q_LORA, KV_LORA, MLA_ROPE, GQA_HEADS * GQA_HD, GQA_KV_HEADS * GQA_HD, GQA_KV_HEADS * GQA_HD])
    zc = lambda n: jnp.zeros((D_MODEL, n), F32)
    win = jnp.concatenate(
        [wi[:, o[0]:o[1]], wi[:, o[1]:o[2]], zc(MLA_NOPE), wi[:, o[2]:o[3]], zc(LANES - MLA_NOPE - MLA_ROPE),
         wi[:, o[3]:o[4]], wi[:, o[4]:o[5]], wi[:, o[5]:o[6]]], axis=1).astype(BF16)
    assert win.shape[1] == _ZW

    wqb = w_q_b[l].reshape(Q_LORA, MLA_HEADS, MLA_NOPE + MLA_ROPE)
    wqb = jnp.pad(wqb, ((0, 0), (0, 0), (0, LANES - MLA_NOPE - MLA_ROPE))).reshape(Q_LORA, MLA_HEADS * LANES)

    wkv = w_kv_b[l].reshape(KV_LORA, MLA_HEADS, MLA_NOPE + MLA_V)
    wk = jnp.pad(wkv[:, :, :MLA_NOPE], ((0, 0), (0, 0), (0, LANES - MLA_NOPE)))
    wv = jnp.pad(wkv[:, :, MLA_NOPE:], ((0, 0), (0, 0), (0, LANES - MLA_V)))
    wkvb = jnp.concatenate([wk, wv], axis=1).reshape(KV_LORA, 2 * MLA_HEADS * LANES)

    gmat = np.kron(np.eye(GQA_HEADS), np.full((GQA_HD, GQA_HD), 1.0 / GQA_HD))
    return dict(
        g1=row(norm_ffn1[l]), wg1=w_ffn1_gate[l].astype(BF16), wu1=w_ffn1_up[l].astype(BF16),
        wd1=w_ffn1_down[l].astype(BF16),
        gmix=row(norm_mix[l]), win=win, gqa=row(q_a_norm[l]), wqb=wqb.astype(BF16), gkv=row(kv_a_norm[l]),
        wkvb=wkvb.astype(BF16),
        gq=row(jnp.tile(gqa_q_norm[l], GQA_HEADS)) * (GQA_HD ** -0.5 * LOG2E),
        gk=row(jnp.tile(gqa_k_norm[l], GQA_KV_HEADS)),
        gmat=jnp.asarray(gmat, BF16), wout=w_out[l].astype(BF16),
        g2=row(norm_ffn2[l]), wg2=w_ffn2_gate[l].astype(BF16), wu2=w_ffn2_up[l].astype(BF16),
        wd2=w_ffn2_down[l].astype(BF16),
    )


def _trunk(x3, layers, tab, final_g):
    batch, seq, _ = x3.shape
    assert seq % ROW_TILE == 0 and seq % Q_TILE == 0 and seq % GRID_W == 0
    x = x3.reshape(batch * seq, D_MODEL)
    for li, p in enumerate(layers):
        x = _ffn_call(x, p["g1"], p["wg1"], p["wu1"], p["wd1"])
        qm, km, vm, qg, kg, vg = _prep_call(x, tab, seq, p)
        o_mla = _attn_call(qm, km, vm, batch, seq, (0, 1), (0, 1), (0, 1), "attn_mla")
        o_gqa = _attn_call(qg, kg, vg, batch, seq, (0, 0, 1, 1), (0, 1, 0, 1), (0, 0, 0, 0), "attn_gqa")
        x = _ffn_call(x, p["g2"], p["wg2"], p["wu2"], p["wd2"], attn=(o_mla, o_gqa, p["wout"]),
                      final_g=final_g if li == len(layers) - 1 else None)
    return x.reshape(batch, seq, D_MODEL)


def kernel(x_prompt, x_sample, norm_ffn1, w_ffn1_gate, w_ffn1_up, w_ffn1_down, norm_mix, w_in, q_a_norm, w_q_b, kv_a_norm, w_kv_b, gqa_q_norm, gqa_k_norm, w_out, norm_ffn2, w_ffn2_gate, w_ffn2_up, w_ffn2_down, final_norm):
    depth = norm_ffn1.shape[0]
    layers = [
        _layer_params(l, norm_ffn1, w_ffn1_gate, w_ffn1_up, w_ffn1_down, norm_mix, w_in, q_a_norm, w_q_b,
                      kv_a_norm, w_kv_b, gqa_q_norm, gqa_k_norm, w_out, norm_ffn2, w_ffn2_gate, w_ffn2_up,
                      w_ffn2_down)
        for l in range(depth)
    ]
    final_g = final_norm.reshape(1, -1).astype(F32)
    outs = []
    for x3 in (x_prompt, x_sample):
        tab = _rope_tables(x3.shape[1])
        outs.append(_trunk(x3, layers, tab, final_g))
    return tuple(outs)
```

```python
import functools

import jax
import jax.numpy as jnp
import numpy as np
from jax import lax
from jax.experimental import pallas as pl
from jax.experimental.pallas import tpu as pltpu

D_MODEL = 1024
GRID_W = 64
ROPE_THETA = 10000.0
EPS = 1e-6

MLA_HEADS = 8
MLA_NOPE = 64
MLA_ROPE = 32
MLA_V = 64
Q_LORA = 256
KV_LORA = 128
GQA_HEADS = 8
GQA_KV_HEADS = 2
GQA_HD = 64
D_FF = 2816

LANES = 128
LOG2E = 1.4426950408889634

ROW_TILE = 1024
FF_CHUNK = 256
Q_TILE = 512
KEY_CHUNK = 512
PV_ROWS = 128
MAX_ROWS = 16
CHUNK_UNROLL = 4
VMEM_LIMIT_BYTES = 56 * 1024 * 1024

_ZW = Q_LORA + KV_LORA + LANES + GQA_HEADS * GQA_HD + 2 * LANES
_Z_CQ = (0, Q_LORA)
_Z_CKV = (Q_LORA, Q_LORA + KV_LORA)
_Z_KPE = (_Z_CKV[1], _Z_CKV[1] + LANES)
_Z_QG = (_Z_KPE[1], _Z_KPE[1] + GQA_HEADS * GQA_HD)
_Z_KG = (_Z_QG[1], _Z_QG[1] + LANES)
_Z_VG = (_Z_KG[1], _Z_KG[1] + LANES)

BF16 = jnp.bfloat16
F32 = jnp.float32


def _rms(x, g):
    ms = jnp.mean(x * x, axis=-1, keepdims=True)
    return x * lax.rsqrt(ms + EPS) * g


def _rope(x, c, sa, sb, d):
    return x * c + pltpu.roll(x, LANES - d, 1) * sa + pltpu.roll(x, d, 1) * sb


def _ffn_kernel(*refs, has_attn, final):
    it = iter(refs)
    x_ref = next(it)
    if has_attn:
        om_ref, og_ref, wo_ref = next(it), next(it), next(it)
    g_ref, wg_ref, wu_ref, wd_ref = next(it), next(it), next(it), next(it)
    if final:
        fg_ref = next(it)
    o_ref = next(it)
    a_ref = next(it)

    x = x_ref[...]
    if has_attn:
        half = om_ref.shape[1]
        x = x + jnp.dot(om_ref[...], wo_ref[:half, :], preferred_element_type=F32)
        x = x + jnp.dot(og_ref[...], wo_ref[half:, :], preferred_element_type=F32)
    hb = _rms(x, g_ref[...]).astype(BF16)
    for c in range(D_FF // FF_CHUNK):
        sl = slice(c * FF_CHUNK, (c + 1) * FF_CHUNK)
        g = jnp.dot(hb, wg_ref[:, sl], preferred_element_type=F32)
        u = jnp.dot(hb, wu_ref[:, sl], preferred_element_type=F32)
        a_ref[:, sl] = (g * jax.nn.sigmoid(g) * u).astype(BF16)
    y = x + 0.5 * jnp.dot(a_ref[...], wd_ref[...], preferred_element_type=F32)
    if final:
        y = _rms(y, fg_ref[...])
    o_ref[...] = y


def _const_spec(shape):
    return pl.BlockSpec(shape, lambda *_: (0,) * len(shape), pipeline_mode=pl.Buffered(1))


def _ffn_call(x, g, wg, wu, wd, attn=None, final_g=None):
    t = x.shape[0]
    row = lambda w: pl.BlockSpec((ROW_TILE, w), lambda i: (i, 0))
    args, specs = [x], [row(D_MODEL)]
    if attn is not None:
        o_mla, o_gqa, w_out = attn
        args += [o_mla, o_gqa, w_out]
        specs += [row(o_mla.shape[1]), row(o_gqa.shape[1]), _const_spec(w_out.shape)]
    args += [g, wg, wu, wd]
    specs += [_const_spec(a.shape) for a in (g, wg, wu, wd)]
    if final_g is not None:
        args.append(final_g)
        specs.append(_const_spec(final_g.shape))
    return pl.pallas_call(
        functools.partial(_ffn_kernel, has_attn=attn is not None, final=final_g is not None),
        grid=(t // ROW_TILE,),
        in_specs=specs,
        out_specs=row(D_MODEL),
        out_shape=jax.ShapeDtypeStruct((t, D_MODEL), F32),
        scratch_shapes=[pltpu.VMEM((ROW_TILE, D_FF), BF16)],
        compiler_params=pltpu.CompilerParams(vmem_limit_bytes=VMEM_LIMIT_BYTES),
        name="ffn_attn" if attn is not None else "ffn",
    )(*args)


def _prep_kernel(x_ref, tab_ref, gmix_ref, win_ref, gqa_ref, wqb_ref, gkv_ref, wkvb_ref, gq_ref, gk_ref,
                 gmat_ref, qm_ref, km_ref, vm_ref, qg_ref, kg_ref, vg_ref):
    hb = _rms(x_ref[...], gmix_ref[...]).astype(BF16)
    z = jnp.dot(hb, win_ref[...], preferred_element_type=F32)

    mc, msa, msb = (tab_ref[:, j * LANES:(j + 1) * LANES] for j in range(3))
    gc, gsa, gsb = (tab_ref[:, j * LANES:(j + 1) * LANES] for j in range(3, 6))
    lane = lax.broadcasted_iota(jnp.int32, (1, LANES), 1)
    lo = lane < GQA_HD
    one = (lane == MLA_V).astype(F32)

    cq = _rms(z[:, _Z_CQ[0]:_Z_CQ[1]], gqa_ref[...]).astype(BF16)
    qa = jnp.dot(cq, wqb_ref[...], preferred_element_type=F32)
    q_scale = (MLA_NOPE + MLA_ROPE) ** -0.5 * LOG2E
    for h in range(MLA_HEADS):
        sl = slice(h * LANES, (h + 1) * LANES)
        qm_ref[:, sl] = (_rope(qa[:, sl], mc, msa, msb, MLA_ROPE // 4) * q_scale).astype(BF16)

    ckv = _rms(z[:, _Z_CKV[0]:_Z_CKV[1]], gkv_ref[...]).astype(BF16)
    kv = jnp.dot(ckv, wkvb_ref[...], preferred_element_type=F32)
    kpe = _rope(z[:, _Z_KPE[0]:_Z_KPE[1]], mc, msa, msb, MLA_ROPE // 4)
    for h in range(MLA_HEADS):
        sl = slice(h * LANES, (h + 1) * LANES)
        km_ref[:, sl] = (kv[:, sl] + kpe).astype(BF16)
        vsl = slice((MLA_HEADS + h) * LANES, (MLA_HEADS + h + 1) * LANES)
        vm_ref[:, sl] = (kv[:, vsl] + one).astype(BF16)

    def head_mean_sq(v, gmat):
        sq = v * v
        hi = sq.astype(BF16)
        lo_part = (sq - hi.astype(F32)).astype(BF16)
        return (jnp.dot(hi, gmat, preferred_element_type=F32)
                + jnp.dot(lo_part, gmat, preferred_element_type=F32))

    qg = z[:, _Z_QG[0]:_Z_QG[1]]
    qg = qg * lax.rsqrt(head_mean_sq(qg, gmat_ref[...]) + EPS) * gq_ref[...]
    for j in range(GQA_HEADS // 2):
        sl = slice(j * LANES, (j + 1) * LANES)
        qg_ref[:, sl] = _rope(qg[:, sl], gc, gsa, gsb, GQA_HD // 4).astype(BF16)

    kg = z[:, _Z_KG[0]:_Z_KG[1]]
    kg = kg * lax.rsqrt(head_mean_sq(kg, gmat_ref[:LANES, :LANES]) + EPS) * gk_ref[...]
    kg = _rope(kg, gc, gsa, gsb, GQA_HD // 4)
    kg_sw = pltpu.roll(kg, GQA_HD, 1)
    zero = jnp.zeros_like(kg)
    kg_ref[:, 0 * LANES:1 * LANES] = jnp.where(lo, kg, zero).astype(BF16)
    kg_ref[:, 1 * LANES:2 * LANES] = jnp.where(lo, zero, kg_sw).astype(BF16)
    kg_ref[:, 2 * LANES:3 * LANES] = jnp.where(lo, kg_sw, zero).astype(BF16)
    kg_ref[:, 3 * LANES:4 * LANES] = jnp.where(lo, zero, kg).astype(BF16)

    vg = z[:, _Z_VG[0]:_Z_VG[1]]
    vg_sw = pltpu.roll(vg, GQA_HD, 1)
    vg_ref[:, 0 * LANES:1 * LANES] = jnp.where(lo, vg, one).astype(BF16)
    vg_ref[:, 1 * LANES:2 * LANES] = jnp.where(lo, vg_sw, one).astype(BF16)


def _prep_call(x, tab, seq, p):
    t = x.shape[0]
    tiles_per_seq = seq // ROW_TILE
    row = lambda w: pl.BlockSpec((ROW_TILE, w), lambda i: (i, 0))
    consts = [p["gmix"], p["win"], p["gqa"], p["wqb"], p["gkv"], p["wkvb"], p["gq"], p["gk"], p["gmat"]]
    widths = (MLA_HEADS * LANES,) * 3 + (GQA_HEADS // 2 * LANES, 2 * GQA_KV_HEADS * LANES, GQA_KV_HEADS * LANES)
    return pl.pallas_call(
        _prep_kernel,
        grid=(t // ROW_TILE,),
        in_specs=[row(D_MODEL), pl.BlockSpec((ROW_TILE, tab.shape[1]), lambda i: (i % tiles_per_seq, 0))]
        + [_const_spec(c.shape) for c in consts],
        out_specs=[row(w) for w in widths],
        out_shape=[jax.ShapeDtypeStruct((t, w), BF16) for w in widths],
        compiler_params=pltpu.CompilerParams(vmem_limit_bytes=VMEM_LIMIT_BYTES),
        name="prep",
    )(x, tab, *consts)


def _attn_kernel(q_ref, k_ref, v_ref, o_ref, vt_ref, sa_ref, sb_ref, acc_ref, *, q_slots, k_slots, v_slots):
    seq = k_ref.shape[0]
    nq, nc, npairs = seq // Q_TILE, seq // KEY_CHUNK, len(q_slots) // 2
    tiles_per_iter = 2 if npairs % 2 else 1
    items_per_iter = npairs * tiles_per_iter
    n_iter = nq // tiles_per_iter
    assert len(q_slots) % 2 == 0 and nq % tiles_per_iter == 0
    s_bufs = (sa_ref, sb_ref)
    neg_inf = [jnp.full((MAX_ROWS, Q_TILE), -jnp.inf, F32)] * 2

    def item(it, e):
        return it * tiles_per_iter + e // npairs, e % npairs, s_bufs[e % 2]

    def score_chunk(tile, pair, s_ref, c, mparts, heads=(0, 1)):
        qrows = pl.ds(pl.multiple_of(tile * Q_TILE, Q_TILE), Q_TILE)
        krows = pl.ds(pl.multiple_of(c * KEY_CHUNK, KEY_CHUNK), KEY_CHUNK)
        out = []
        for e in heads:
            j = 2 * pair + e
            q = q_ref[qrows, q_slots[j] * LANES:(q_slots[j] + 1) * LANES]
            k = k_ref[krows, k_slots[j] * LANES:(k_slots[j] + 1) * LANES]
            st = lax.dot_general(k, q, (((1,), (1,)), ((), ())), preferred_element_type=F32)
            s_ref[e, krows, :] = st
            part = jnp.max(st.reshape(KEY_CHUNK // MAX_ROWS, MAX_ROWS, Q_TILE), axis=0)
            out.append(jnp.maximum(mparts[e], part))
        return out

    def pv_chunk(pair, s_ref, c, m, heads=(0, 1)):
        krows = pl.ds(pl.multiple_of(c * KEY_CHUNK, KEY_CHUNK), KEY_CHUNK)
        for e in heads:
            pt = jnp.exp2((s_ref[e, krows, :] - m[e]).astype(BF16))
            acc_ref[e] += jnp.dot(vt_ref[v_slots[2 * pair + e], c], pt, preferred_element_type=F32)

    def col_max(mparts):
        return [jnp.max(mp, axis=0, keepdims=True) for mp in mparts]

    def finish(tile, pair):
        halves = [acc_ref[e, :MLA_V, :] / acc_ref[e, MLA_V:MLA_V + 1, :] for e in range(2)]
        ot = jnp.concatenate(halves, axis=0)
        rows = pl.ds(pl.multiple_of(tile * Q_TILE, Q_TILE), Q_TILE)
        o_ref[rows, pair * LANES:(pair + 1) * LANES] = ot.astype(BF16).T
        acc_ref[...] = jnp.zeros_like(acc_ref)

    def item_step(it, e, m_prev):
        tile, pair, s_cur = item(it, e)
        tile_p, pair_p, s_prev = item(it, e - 1) if e > 0 else item(it - 1, items_per_iter - 1)

        def chunk(c, mp):
            out = []
            for e in range(2):
                out.append(score_chunk(tile, pair, s_cur, c, mp, heads=(e,))[0])
                pv_chunk(pair_p, s_prev, c, m_prev, heads=(e,))
            return out

        mparts = lax.fori_loop(0, nc, chunk, neg_inf, unroll=CHUNK_UNROLL)
        finish(tile_p, pair_p)
        return col_max(mparts)

    def trip(it, m, first):
        for e in range(first, items_per_iter):
            m = item_step(it, e, m)
        return m

    acc_ref[...] = jnp.zeros_like(acc_ref)
    tile0, pair0, s0 = item(0, 0)

    def first_chunk(c, mp):
        krows = pl.ds(pl.multiple_of(c * KEY_CHUNK, KEY_CHUNK), KEY_CHUNK)
        for slot in sorted(set(v_slots)):
            vt_ref[slot, c] = v_ref[krows, slot * LANES:(slot + 1) * LANES].T[:PV_ROWS, :]
        return score_chunk(tile0, pair0, s0, c, mp)

    m = col_max(lax.fori_loop(0, nc, first_chunk, neg_inf, unroll=CHUNK_UNROLL))
    m = trip(0, m, 1)
    m = lax.fori_loop(1, n_iter, lambda it, m: trip(it, m, 0), m)

    tile_l, pair_l, s_l = item(n_iter - 1, items_per_iter - 1)

    def last_chunk(c, carry):
        pv_chunk(pair_l, s_l, c, m)
        return carry

    lax.fori_loop(0, nc, last_chunk, 0, unroll=CHUNK_UNROLL)
    finish(tile_l, pair_l)


def _attn_call(q, k, v, batch, seq, q_slots, k_slots, v_slots, name):
    nh = len(q_slots)
    qw, kw, vw = ((max(s) + 1) * LANES for s in (q_slots, k_slots, v_slots))
    ow = nh // 2 * LANES
    steps = q.shape[1] // qw
    assert k.shape[1] == steps * kw and v.shape[1] == steps * vw
    return pl.pallas_call(
        functools.partial(_attn_kernel, q_slots=q_slots, k_slots=k_slots, v_slots=v_slots),
        grid=(batch, steps),
        in_specs=[
            pl.BlockSpec((seq, qw), lambda b, p: (b, p)),
            pl.BlockSpec((seq, kw), lambda b, p: (b, p)),
            pl.BlockSpec((seq, vw), lambda b, p: (b, p)),
        ],
        out_specs=pl.BlockSpec((seq, ow), lambda b, p: (b, p)),
        out_shape=jax.ShapeDtypeStruct((q.shape[0], steps * ow), BF16),
        scratch_shapes=[
            pltpu.VMEM((vw // LANES, seq // KEY_CHUNK, PV_ROWS, KEY_CHUNK), BF16),
            pltpu.VMEM((2, seq, Q_TILE), F32),
            pltpu.VMEM((2, seq, Q_TILE), F32),
            pltpu.VMEM((2, PV_ROWS, Q_TILE), F32),
        ],
        compiler_params=pltpu.CompilerParams(vmem_limit_bytes=VMEM_LIMIT_BYTES),
        name=name,
    )(q, k, v)


def _rope_tables(seq):
    rows = seq // GRID_W
    row = jnp.repeat(jnp.arange(rows, dtype=F32), GRID_W)
    col = jnp.tile(jnp.arange(GRID_W, dtype=F32), rows)

    def slot_tables(dim, lead, slot_w):
        half = dim // 2
        inv = ROPE_THETA ** (-jnp.arange(0, half, 2, dtype=F32) / half)
        ar, ac = row[:, None] * inv[None, :], col[:, None] * inv[None, :]
        z = jnp.zeros_like(ar)
        c = jnp.concatenate([jnp.cos(ar)] * 2 + [jnp.cos(ac)] * 2, axis=1)
        sa = jnp.concatenate([-jnp.sin(ar), z, -jnp.sin(ac), z], axis=1)
        sb = jnp.concatenate([z, jnp.sin(ar), z, jnp.sin(ac)], axis=1)
        tail = slot_w - lead - dim
        pad = lambda a, fill: jnp.concatenate(
            [jnp.full((seq, lead), fill, F32), a, jnp.full((seq, tail), fill, F32)], axis=1)
        return pad(c, 1.0), pad(sa, 0.0), pad(sb, 0.0)

    mla = slot_tables(MLA_ROPE, MLA_NOPE, LANES)
    gqa = [jnp.tile(a, (1, LANES // GQA_HD)) for a in slot_tables(GQA_HD, 0, GQA_HD)]
    return jnp.concatenate(list(mla) + gqa, axis=1)


def _layer_params(l, norm_ffn1, w_ffn1_gate, w_ffn1_up, w_ffn1_down, norm_mix, w_in, q_a_norm, w_q_b,
                  kv_a_norm, w_kv_b, gqa_q_norm, gqa_k_norm, w_out, norm_ffn2, w_ffn2_gate, w_ffn2_up,
                  w_ffn2_down):
    row = lambda v: v.reshape(1, -1).astype(F32)
    wi = w_in[l]
    o = np.cumsum([0, Q_LORA, KV_LORA, MLA_ROPE, GQA_HEADS * GQA_HD, GQA_KV_HEADS * GQA_HD, GQA_KV_HEADS * GQA_HD])
    zc = lambda n: jnp.zeros((D_MODEL, n), F32)
    win = jnp.concatenate(
        [wi[:, o[0]:o[1]], wi[:, o[1]:o[2]], zc(MLA_NOPE), wi[:, o[2]:o[3]], zc(LANES - MLA_NOPE - MLA_ROPE),
         wi[:, o[3]:o[4]], wi[:, o[4]:o[5]], wi[:, o[5]:o[6]]], axis=1).astype(BF16)
    assert win.shape[1] == _ZW

    wqb = w_q_b[l].reshape(Q_LORA, MLA_HEADS, MLA_NOPE + MLA_ROPE)
    wqb = jnp.pad(wqb, ((0, 0), (0, 0), (0, LANES - MLA_NOPE - MLA_ROPE))).reshape(Q_LORA, MLA_HEADS * LANES)

    wkv = w_kv_b[l].reshape(KV_LORA, MLA_HEADS, MLA_NOPE + MLA_V)
    wk = jnp.pad(wkv[:, :, :MLA_NOPE], ((0, 0), (0, 0), (0, LANES - MLA_NOPE)))
    wv = jnp.pad(wkv[:, :, MLA_NOPE:], ((0, 0), (0, 0), (0, LANES - MLA_V)))
    wkvb = jnp.concatenate([wk, wv], axis=1).reshape(KV_LORA, 2 * MLA_HEADS * LANES)

    gmat = np.kron(np.eye(GQA_HEADS), np.full((GQA_HD, GQA_HD), 1.0 / GQA_HD))
    return dict(
        g1=row(norm_ffn1[l]), wg1=w_ffn1_gate[l].astype(BF16), wu1=w_ffn1_up[l].astype(BF16),
        wd1=w_ffn1_down[l].astype(BF16),
        gmix=row(norm_mix[l]), win=win, gqa=row(q_a_norm[l]), wqb=wqb.astype(BF16), gkv=row(kv_a_norm[l]),
        wkvb=wkvb.astype(BF16),
        gq=row(jnp.tile(gqa_q_norm[l], GQA_HEADS)) * (GQA_HD ** -0.5 * LOG2E),
        gk=row(jnp.tile(gqa_k_norm[l], GQA_KV_HEADS)),
        gmat=jnp.asarray(gmat, BF16), wout=w_out[l].astype(BF16),
        g2=row(norm_ffn2[l]), wg2=w_ffn2_gate[l].astype(BF16), wu2=w_ffn2_up[l].astype(BF16),
        wd2=w_ffn2_down[l].astype(BF16),
    )


def _trunk(x3, layers, tab, final_g):
    batch, seq, _ = x3.shape
    assert seq % ROW_TILE == 0 and seq % Q_TILE == 0 and seq % GRID_W == 0
    x = x3.reshape(batch * seq, D_MODEL)
    for li, p in enumerate(layers):
        x = _ffn_call(x, p["g1"], p["wg1"], p["wu1"], p["wd1"])
        qm, km, vm, qg, kg, vg = _prep_call(x, tab, seq, p)
        o_mla = _attn_call(qm, km, vm, batch, seq, (0, 1), (0, 1), (0, 1), "attn_mla")
        o_gqa = _attn_call(qg, kg, vg, batch, seq, (0, 0, 1, 1), (0, 1, 0, 1), (0, 0, 0, 0), "attn_gqa")
        x = _ffn_call(x, p["g2"], p["wg2"], p["wu2"], p["wd2"], attn=(o_mla, o_gqa, p["wout"]),
                      final_g=final_g if li == len(layers) - 1 else None)
    return x.reshape(batch, seq, D_MODEL)


def kernel(x_prompt, x_sample, norm_ffn1, w_ffn1_gate, w_ffn1_up, w_ffn1_down, norm_mix, w_in, q_a_norm, w_q_b, kv_a_norm, w_kv_b, gqa_q_norm, gqa_k_norm, w_out, norm_ffn2, w_ffn2_gate, w_ffn2_up, w_ffn2_down, final_norm):
    depth = norm_ffn1.shape[0]
    layers = [
        _layer_params(l, norm_ffn1, w_ffn1_gate, w_ffn1_up, w_ffn1_down, norm_mix, w_in, q_a_norm, w_q_b,
                      kv_a_norm, w_kv_b, gqa_q_norm, gqa_k_norm, w_out, norm_ffn2, w_ffn2_gate, w_ffn2_up,
                      w_ffn2_down)
        for l in range(depth)
    ]
    final_g = final_norm.reshape(1, -1).astype(F32)
    outs = []
    for x3 in (x_prompt, x_sample):
        tab = _rope_tables(x3.shape[1])
        outs.append(_trunk(x3, layers, tab, final_g))
    return tuple(outs)
```

```python
import functools

import jax
import jax.numpy as jnp
import numpy as np
from jax import lax
from jax.experimental import pallas as pl
from jax.experimental.pallas import tpu as pltpu

D_MODEL = 1024
GRID_W = 64
ROPE_THETA = 10000.0
EPS = 1e-6

MLA_HEADS = 8
MLA_NOPE = 64
MLA_ROPE = 32
MLA_V = 64
Q_LORA = 256
KV_LORA = 128
GQA_HEADS = 8
GQA_KV_HEADS = 2
GQA_HD = 64
D_FF = 2816

LANES = 128
LOG2E = 1.4426950408889634

ROW_TILE = 1024
FF_CHUNK = 256
Q_TILE = 512
KEY_CHUNK = 512
PV_ROWS = 128
MAX_ROWS = 16
CHUNK_UNROLL = 4
VMEM_LIMIT_BYTES = 56 * 1024 * 1024

_ZW = Q_LORA + KV_LORA + LANES + GQA_HEADS * GQA_HD + 2 * LANES
_Z_CQ = (0, Q_LORA)
_Z_CKV = (Q_LORA, Q_LORA + KV_LORA)
_Z_KPE = (_Z_CKV[1], _Z_CKV[1] + LANES)
_Z_QG = (_Z_KPE[1], _Z_KPE[1] + GQA_HEADS * GQA_HD)
_Z_KG = (_Z_QG[1], _Z_QG[1] + LANES)
_Z_VG = (_Z_KG[1], _Z_KG[1] + LANES)

BF16 = jnp.bfloat16
F32 = jnp.float32


def _rms(x, g):
    ms = jnp.mean(x * x, axis=-1, keepdims=True)
    return x * lax.rsqrt(ms + EPS) * g


def _rope(x, c, sa, sb, d):
    return x * c + pltpu.roll(x, LANES - d, 1) * sa + pltpu.roll(x, d, 1) * sb


def _ffn_kernel(*refs, has_attn, final):
    it = iter(refs)
    x_ref = next(it)
    if has_attn:
        om_ref, og_ref, wo_ref = next(it), next(it), next(it)
    g_ref, wg_ref, wu_ref, wd_ref = next(it), next(it), next(it), next(it)
    if final:
        fg_ref = next(it)
    o_ref = next(it)
    a_ref = next(it)

    x = x_ref[...]
    if has_attn:
        half = om_ref.shape[1]
        x = x + jnp.dot(om_ref[...], wo_ref[:half, :], preferred_element_type=F32)
        x = x + jnp.dot(og_ref[...], wo_ref[half:, :], preferred_element_type=F32)
    hb = _rms(x, g_ref[...]).astype(BF16)
    for c in range(D_FF // FF_CHUNK):
        sl = slice(c * FF_CHUNK, (c + 1) * FF_CHUNK)
        g = jnp.dot(hb, wg_ref[:, sl], preferred_element_type=F32)
        u = jnp.dot(hb, wu_ref[:, sl], preferred_element_type=F32)
        a_ref[:, sl] = (g * jax.nn.sigmoid(g) * u).astype(BF16)
    y = x + 0.5 * jnp.dot(a_ref[...], wd_ref[...], preferred_element_type=F32)
    if final:
        y = _rms(y, fg_ref[...])
    o_ref[...] = y


def _const_spec(shape):
    return pl.BlockSpec(shape, lambda *_: (0,) * len(shape), pipeline_mode=pl.Buffered(1))


def _ffn_call(x, g, wg, wu, wd, attn=None, final_g=None):
    t = x.shape[0]
    row = lambda w: pl.BlockSpec((ROW_TILE, w), lambda i: (i, 0))
    args, specs = [x], [row(D_MODEL)]
    if attn is not None:
        o_mla, o_gqa, w_out = attn
        args += [o_mla, o_gqa, w_out]
        specs += [row(o_mla.shape[1]), row(o_gqa.shape[1]), _const_spec(w_out.shape)]
    args += [g, wg, wu, wd]
    specs += [_const_spec(a.shape) for a in (g, wg, wu, wd)]
    if final_g is not None:
        args.append(final_g)
        specs.append(_const_spec(final_g.shape))
    return pl.pallas_call(
        functools.partial(_ffn_kernel, has_attn=attn is not None, final=final_g is not None),
        grid=(t // ROW_TILE,),
        in_specs=specs,
        out_specs=row(D_MODEL),
        out_shape=jax.ShapeDtypeStruct((t, D_MODEL), F32),
        scratch_shapes=[pltpu.VMEM((ROW_TILE, D_FF), BF16)],
        compiler_params=pltpu.CompilerParams(vmem_limit_bytes=VMEM_LIMIT_BYTES),
        name="ffn_attn" if attn is not None else "ffn",
    )(*args)


def _prep_kernel(x_ref, tab_ref, gmix_ref, win_ref, gqa_ref, wqb_ref, gkv_ref, wkvb_ref, gq_ref, gk_ref,
                 gmat_ref, qm_ref, km_ref, vm_ref, qg_ref, kg_ref, vg_ref):
    hb = _rms(x_ref[...], gmix_ref[...]).astype(BF16)
    z = jnp.dot(hb, win_ref[...], preferred_element_type=F32)

    mc, msa, msb = (tab_ref[:, j * LANES:(j + 1) * LANES] for j in range(3))
    gc, gsa, gsb = (tab_ref[:, j * LANES:(j + 1) * LANES] for j in range(3, 6))
    lane = lax.broadcasted_iota(jnp.int32, (1, LANES), 1)
    lo = lane < GQA_HD
    one = (lane == MLA_V).astype(F32)

    cq = _rms(z[:, _Z_CQ[0]:_Z_CQ[1]], gqa_ref[...]).astype(BF16)
    qa = jnp.dot(cq, wqb_ref[...], preferred_element_type=F32)
    q_scale = (MLA_NOPE + MLA_ROPE) ** -0.5 * LOG2E
    for h in range(MLA_HEADS):
        sl = slice(h * LANES, (h + 1) * LANES)
        qm_ref[:, sl] = (_rope(qa[:, sl], mc, msa, msb, MLA_ROPE // 4) * q_scale).astype(BF16)

    ckv = _rms(z[:, _Z_CKV[0]:_Z_CKV[1]], gkv_ref[...]).astype(BF16)
    kv = jnp.dot(ckv, wkvb_ref[...], preferred_element_type=F32)
    kpe = _rope(z[:, _Z_KPE[0]:_Z_KPE[1]], mc, msa, msb, MLA_ROPE // 4)
    for h in range(MLA_HEADS):
        sl = slice(h * LANES, (h + 1) * LANES)
        km_ref[:, sl] = (kv[:, sl] + kpe).astype(BF16)
        vsl = slice((MLA_HEADS + h) * LANES, (MLA_HEADS + h + 1) * LANES)
        vm_ref[:, sl] = (kv[:, vsl] + one).astype(BF16)

    def head_mean_sq(v, gmat):
        sq = v * v
        hi = sq.astype(BF16)
        lo_part = (sq - hi.astype(F32)).astype(BF16)
        return (jnp.dot(hi, gmat, preferred_element_type=F32)
                + jnp.dot(lo_part, gmat, preferred_element_type=F32))

    qg = z[:, _Z_QG[0]:_Z_QG[1]]
    qg = qg * lax.rsqrt(head_mean_sq(qg, gmat_ref[...]) + EPS) * gq_ref[...]
    for j in range(GQA_HEADS // 2):
        sl = slice(j * LANES, (j + 1) * LANES)
        qg_ref[:, sl] = _rope(qg[:, sl], gc, gsa, gsb, GQA_HD // 4).astype(BF16)

    kg = z[:, _Z_KG[0]:_Z_KG[1]]
    kg = kg * lax.rsqrt(head_mean_sq(kg, gmat_ref[:LANES, :LANES]) + EPS) * gk_ref[...]
    kg = _rope(kg, gc, gsa, gsb, GQA_HD // 4)
    kg_sw = pltpu.roll(kg, GQA_HD, 1)
    zero = jnp.zeros_like(kg)
    kg_ref[:, 0 * LANES:1 * LANES] = jnp.where(lo, kg, zero).astype(BF16)
    kg_ref[:, 1 * LANES:2 * LANES] = jnp.where(lo, zero, kg_sw).astype(BF16)
    kg_ref[:, 2 * LANES:3 * LANES] = jnp.where(lo, kg_sw, zero).astype(BF16)
    kg_ref[:, 3 * LANES:4 * LANES] = jnp.where(lo, zero, kg).astype(BF16)

    vg = z[:, _Z_VG[0]:_Z_VG[1]]
    vg_sw = pltpu.roll(vg, GQA_HD, 1)
    vg_ref[:, 0 * LANES:1 * LANES] = jnp.where(lo, vg, one).astype(BF16)
    vg_ref[:, 1 * LANES:2 * LANES] = jnp.where(lo, vg_sw, one).astype(BF16)


def _prep_call(x, tab, seq, p):
    t = x.shape[0]
    tiles_per_seq = seq // ROW_TILE
    row = lambda w: pl.BlockSpec((ROW_TILE, w), lambda i: (i, 0))
    consts = [p["gmix"], p["win"], p["gqa"], p["wqb"], p["gkv"], p["wkvb"], p["gq"], p["gk"], p["gmat"]]
    widths = (MLA_HEADS * LANES,) * 3 + (GQA_HEADS // 2 * LANES, 2 * GQA_KV_HEADS * LANES, GQA_KV_HEADS * LANES)
    return pl.pallas_call(
        _prep_kernel,
        grid=(t // ROW_TILE,),
        in_specs=[row(D_MODEL), pl.BlockSpec((ROW_TILE, tab.shape[1]), lambda i: (i % tiles_per_seq, 0))]
        + [_const_spec(c.shape) for c in consts],
        out_specs=[row(w) for w in widths],
        out_shape=[jax.ShapeDtypeStruct((t, w), BF16) for w in widths],
        compiler_params=pltpu.CompilerParams(vmem_limit_bytes=VMEM_LIMIT_BYTES),
        name="prep",
    )(x, tab, *consts)


def _attn_kernel(q_ref, k_ref, v_ref, o_ref, vt_ref, sa_ref, sb_ref, acc_ref, *, q_slots, k_slots, v_slots):
    seq = k_ref.shape[0]
    nq, nc, npairs = seq // Q_TILE, seq // KEY_CHUNK, len(q_slots) // 2
    tiles_per_iter = 2 if npairs % 2 else 1
    items_per_iter = npairs * tiles_per_iter
    n_iter = nq // tiles_per_iter
    assert len(q_slots) % 2 == 0 and nq % tiles_per_iter == 0
    s_bufs = (sa_ref, sb_ref)
    neg_inf = [jnp.full((MAX_ROWS, Q_TILE), -jnp.inf, F32)] * 2

    def item(it, e):
        return it * tiles_per_iter + e // npairs, e % npairs, s_bufs[e % 2]

    def score_chunk(tile, pair, s_ref, c, mparts, heads=(0, 1)):
        qrows = pl.ds(pl.multiple_of(tile * Q_TILE, Q_TILE), Q_TILE)
        krows = pl.ds(pl.multiple_of(c * KEY_CHUNK, KEY_CHUNK), KEY_CHUNK)
        out = []
        for e in heads:
            j = 2 * pair + e
            q = q_ref[qrows, q_slots[j] * LANES:(q_slots[j] + 1) * LANES]
            k = k_ref[krows, k_slots[j] * LANES:(k_slots[j] + 1) * LANES]
            st = lax.dot_general(k, q, (((1,), (1,)), ((), ())), preferred_element_type=F32)
            s_ref[e, krows, :] = st
            part = jnp.max(st.reshape(KEY_CHUNK // MAX_ROWS, MAX_ROWS, Q_TILE), axis=0)
            out.append(jnp.maximum(mparts[e], part))
        return out

    def pv_chunk(pair, s_ref, c, m, heads=(0, 1)):
        krows = pl.ds(pl.multiple_of(c * KEY_CHUNK, KEY_CHUNK), KEY_CHUNK)
        for e in heads:
            pt = jnp.exp2((s_ref[e, krows, :] - m[e]).astype(BF16))
            acc_ref[e] += jnp.dot(vt_ref[v_slots[2 * pair + e], c], pt, preferred_element_type=F32)

    def col_max(mparts):
        return [jnp.max(mp, axis=0, keepdims=True) for mp in mparts]

    def finish(tile, pair):
        halves = [acc_ref[e, :MLA_V, :] / acc_ref[e, MLA_V:MLA_V + 1, :] for e in range(2)]
        ot = jnp.concatenate(halves, axis=0)
        rows = pl.ds(pl.multiple_of(tile * Q_TILE, Q_TILE), Q_TILE)
        o_ref[rows, pair * LANES:(pair + 1) * LANES] = ot.T.astype(BF16)
        acc_ref[...] = jnp.zeros_like(acc_ref)

    def item_step(it, e, m_prev):
        tile, pair, s_cur = item(it, e)
        tile_p, pair_p, s_prev = item(it, e - 1) if e > 0 else item(it - 1, items_per_iter - 1)

        def chunk(c, mp):
            out = []
            for e in range(2):
                out.append(score_chunk(tile, pair, s_cur, c, mp, heads=(e,))[0])
                pv_chunk(pair_p, s_prev, c, m_prev, heads=(e,))
            return out

        mparts = lax.fori_loop(0, nc, chunk, neg_inf, unroll=CHUNK_UNROLL)
        finish(tile_p, pair_p)
        return col_max(mparts)

    def trip(it, m, first):
        for e in range(first, items_per_iter):
            m = item_step(it, e, m)
        return m

    acc_ref[...] = jnp.zeros_like(acc_ref)
    tile0, pair0, s0 = item(0, 0)

    def first_chunk(c, mp):
        krows = pl.ds(pl.multiple_of(c * KEY_CHUNK, KEY_CHUNK), KEY_CHUNK)
        for slot in sorted(set(v_slots)):
            vt_ref[slot, c] = v_ref[krows, slot * LANES:(slot + 1) * LANES].T[:PV_ROWS, :]
        return score_chunk(tile0, pair0, s0, c, mp)

    m = col_max(lax.fori_loop(0, nc, first_chunk, neg_inf, unroll=CHUNK_UNROLL))
    m = trip(0, m, 1)
    m = lax.fori_loop(1, n_iter, lambda it, m: trip(it, m, 0), m)

    tile_l, pair_l, s_l = item(n_iter - 1, items_per_iter - 1)

    def last_chunk(c, carry):
        pv_chunk(pair_l, s_l, c, m)
        return carry

    lax.fori_loop(0, nc, last_chunk, 0, unroll=CHUNK_UNROLL)
    finish(tile_l, pair_l)


def _attn_call(q, k, v, batch, seq, q_slots, k_slots, v_slots, name):
    nh = len(q_slots)
    qw, kw, vw = ((max(s) + 1) * LANES for s in (q_slots, k_slots, v_slots))
    ow = nh // 2 * LANES
    steps = q.shape[1] // qw
    assert k.shape[1] == steps * kw and v.shape[1] == steps * vw
    return pl.pallas_call(
        functools.partial(_attn_kernel, q_slots=q_slots, k_slots=k_slots, v_slots=v_slots),
        grid=(batch, steps),
        in_specs=[
            pl.BlockSpec((seq, qw), lambda b, p: (b, p)),
            pl.BlockSpec((seq, kw), lambda b, p: (b, p)),
            pl.BlockSpec((seq, vw), lambda b, p: (b, p)),
        ],
        out_specs=pl.BlockSpec((seq, ow), lambda b, p: (b, p)),
        out_shape=jax.ShapeDtypeStruct((q.shape[0], steps * ow), BF16),
        scratch_shapes=[
            pltpu.VMEM((vw // LANES, seq // KEY_CHUNK, PV_ROWS, KEY_CHUNK), BF16),
            pltpu.VMEM((2, seq, Q_TILE), F32),
            pltpu.VMEM((2, seq, Q_TILE), F32),
            pltpu.VMEM((2, PV_ROWS, Q_TILE), F32),
        ],
        compiler_params=pltpu.CompilerParams(vmem_limit_bytes=VMEM_LIMIT_BYTES),
        name=name,
    )(q, k, v)


def _rope_tables(seq):
    rows = seq // GRID_W
    row = jnp.repeat(jnp.arange(rows, dtype=F32), GRID_W)
    col = jnp.tile(jnp.arange(GRID_W, dtype=F32), rows)

    def slot_tables(dim, lead, slot_w):
        half = dim // 2
        inv = ROPE_THETA ** (-jnp.arange(0, half, 2, dtype=F32) / half)
        ar, ac = row[:, None] * inv[None, :], col[:, None] * inv[None, :]
        z = jnp.zeros_like(ar)
        c = jnp.concatenate([jnp.cos(ar)] * 2 + [jnp.cos(ac)] * 2, axis=1)
        sa = jnp.concatenate([-jnp.sin(ar), z, -jnp.sin(ac), z], axis=1)
        sb = jnp.concatenate([z, jnp.sin(ar), z, jnp.sin(ac)], axis=1)
        tail = slot_w - lead - dim
        pad = lambda a, fill: jnp.concatenate(
            [jnp.full((seq, lead), fill, F32), a, jnp.full((seq, tail), fill, F32)], axis=1)
        return pad(c, 1.0), pad(sa, 0.0), pad(sb, 0.0)

    mla = slot_tables(MLA_ROPE, MLA_NOPE, LANES)
    gqa = [jnp.tile(a, (1, LANES // GQA_HD)) for a in slot_tables(GQA_HD, 0, GQA_HD)]
    return jnp.concatenate(list(mla) + gqa, axis=1)


def _layer_params(l, norm_ffn1, w_ffn1_gate, w_ffn1_up, w_ffn1_down, norm_mix, w_in, q_a_norm, w_q_b,
                  kv_a_norm, w_kv_b, gqa_q_norm, gqa_k_norm, w_out, norm_ffn2, w_ffn2_gate, w_ffn2_up,
                  w_ffn2_down):
    row = lambda v: v.reshape(1, -1).astype(F32)
    wi = w_in[l]
    o = np.cumsum([0, Q_LORA, KV_LORA, MLA_ROPE, GQA_HEADS * GQA_HD, GQA_KV_HEADS * GQA_HD, GQA_KV_HEADS * GQA_HD])
    zc = lambda n: jnp.zeros((D_MODEL, n), F32)
    win = jnp.concatenate(
        [wi[:, o[0]:o[1]], wi[:, o[1]:o[2]], zc(MLA_NOPE), wi[:, o[2]:o[3]], zc(LANES - MLA_NOPE - MLA_ROPE),
         wi[:, o[3]:o[4]], wi[:, o[4]:o[5]], wi[:, o[5]:o[6]]], axis=1).astype(BF16)
    assert win.shape[1] == _ZW

    wqb = w_q_b[l].reshape(Q_LORA, MLA_HEADS, MLA_NOPE + MLA_ROPE)
    wqb = jnp.pad(wqb, ((0, 0), (0, 0), (0, LANES - MLA_NOPE - MLA_ROPE))).reshape(Q_LORA, MLA_HEADS * LANES)

    wkv = w_kv_b[l].reshape(KV_LORA, MLA_HEADS, MLA_NOPE + MLA_V)
    wk = jnp.pad(wkv[:, :, :MLA_NOPE], ((0, 0), (0, 0), (0, LANES - MLA_NOPE)))
    wv = jnp.pad(wkv[:, :, MLA_NOPE:], ((0, 0), (0, 0), (0, LANES - MLA_V)))
    wkvb = jnp.concatenate([wk, wv], axis=1).reshape(KV_LORA, 2 * MLA_HEADS * LANES)

    gmat = np.kron(np.eye(GQA_HEADS), np.full((GQA_HD, GQA_HD), 1.0 / GQA_HD))
    return dict(
        g1=row(norm_ffn1[l]), wg1=w_ffn1_gate[l].astype(BF16), wu1=w_ffn1_up[l].astype(BF16),
        wd1=w_ffn1_down[l].astype(BF16),
        gmix=row(norm_mix[l]), win=win, gqa=row(q_a_norm[l]), wqb=wqb.astype(BF16), gkv=row(kv_a_norm[l]),
        wkvb=wkvb.astype(BF16),
        gq=row(jnp.tile(gqa_q_norm[l], GQA_HEADS)) * (GQA_HD ** -0.5 * LOG2E),
        gk=row(jnp.tile(gqa_k_norm[l], GQA_KV_HEADS)),
        gmat=jnp.asarray(gmat, BF16), wout=w_out[l].astype(BF16),
        g2=row(norm_ffn2[l]), wg2=w_ffn2_gate[l].astype(BF16), wu2=w_ffn2_up[l].astype(BF16),
        wd2=w_ffn2_down[l].astype(BF16),
    )


def _trunk(x3, layers, tab, final_g):
    batch, seq, _ = x3.shape
    assert seq % ROW_TILE == 0 and seq % Q_TILE == 0 and seq % GRID_W == 0
    x = x3.reshape(batch * seq, D_MODEL)
    for li, p in enumerate(layers):
        x = _ffn_call(x, p["g1"], p["wg1"], p["wu1"], p["wd1"])
        qm, km, vm, qg, kg, vg = _prep_call(x, tab, seq, p)
        o_mla = _attn_call(qm, km, vm, batch, seq, (0, 1), (0, 1), (0, 1), "attn_mla")
        o_gqa = _attn_call(qg, kg, vg, batch, seq, (0, 0, 1, 1), (0, 1, 0, 1), (0, 0, 0, 0), "attn_gqa")
        x = _ffn_call(x, p["g2"], p["wg2"], p["wu2"], p["wd2"], attn=(o_mla, o_gqa, p["wout"]),
                      final_g=final_g if li == len(layers) - 1 else None)
    return x.reshape(batch, seq, D_MODEL)


def kernel(x_prompt, x_sample, norm_ffn1, w_ffn1_gate, w_ffn1_up, w_ffn1_down, norm_mix, w_in, q_a_norm, w_q_b, kv_a_norm, w_kv_b, gqa_q_norm, gqa_k_norm, w_out, norm_ffn2, w_ffn2_gate, w_ffn2_up, w_ffn2_down, final_norm):
    depth = norm_ffn1.shape[0]
    layers = [
        _layer_params(l, norm_ffn1, w_ffn1_gate, w_ffn1_up, w_ffn1_down, norm_mix, w_in, q_a_norm, w_q_b,
                      kv_a_norm, w_kv_b, gqa_q_norm, gqa_k_norm, w_out, norm_ffn2, w_ffn2_gate, w_ffn2_up,
                      w_ffn2_down)
        for l in range(depth)
    ]
    final_g = final_norm.reshape(1, -1).astype(F32)
    outs = []
    for x3 in (x_prompt, x_sample):
        tab = _rope_tables(x3.shape[1])
        outs.append(_trunk(x3, layers, tab, final_g))
    return tuple(outs)
```

```python
import functools

import jax
import jax.numpy as jnp
import numpy as np
from jax import lax
from jax.experimental import pallas as pl
from jax.experimental.pallas import tpu as pltpu

D_MODEL = 1024
GRID_W = 64
ROPE_THETA = 10000.0
EPS = 1e-6

MLA_HEADS = 8
MLA_NOPE = 64
MLA_ROPE = 32
MLA_V = 64
Q_LORA = 256
KV_LORA = 128
GQA_HEADS = 8
GQA_KV_HEADS = 2
GQA_HD = 64
D_FF = 2816

LANES = 128
LOG2E = 1.4426950408889634

ROW_TILE = 1024
FF_CHUNK = 256
Q_TILE = 512
KEY_CHUNK = 512
PV_ROWS = 128
MAX_ROWS = 8
CHUNK_UNROLL = 4
VMEM_LIMIT_BYTES = 56 * 1024 * 1024

_ZW = Q_LORA + KV_LORA + LANES + GQA_HEADS * GQA_HD + 2 * LANES
_Z_CQ = (0, Q_LORA)
_Z_CKV = (Q_LORA, Q_LORA + KV_LORA)
_Z_KPE = (_Z_CKV[1], _Z_CKV[1] + LANES)
_Z_QG = (_Z_KPE[1], _Z_KPE[1] + GQA_HEADS * GQA_HD)
_Z_KG = (_Z_QG[1], _Z_QG[1] + LANES)
_Z_VG = (_Z_KG[1], _Z_KG[1] + LANES)

BF16 = jnp.bfloat16
F32 = jnp.float32


def _rms(x, g):
    ms = jnp.mean(x * x, axis=-1, keepdims=True)
    return x * lax.rsqrt(ms + EPS) * g


def _rope(x, c, sa, sb, d):
    return x * c + pltpu.roll(x, LANES - d, 1) * sa + pltpu.roll(x, d, 1) * sb


def _ffn_kernel(*refs, has_attn, final):
    it = iter(refs)
    x_ref = next(it)
    if has_attn:
        om_ref, og_ref, wo_ref = next(it), next(it), next(it)
    g_ref, wg_ref, wu_ref, wd_ref = next(it), next(it), next(it), next(it)
    if final:
        fg_ref = next(it)
    o_ref = next(it)
    a_ref = next(it)

    x = x_ref[...]
    if has_attn:
        half = om_ref.shape[1]
        x = x + jnp.dot(om_ref[...], wo_ref[:half, :], preferred_element_type=F32)
        x = x + jnp.dot(og_ref[...], wo_ref[half:, :], preferred_element_type=F32)
    hb = _rms(x, g_ref[...]).astype(BF16)
    for c in range(D_FF // FF_CHUNK):
        sl = slice(c * FF_CHUNK, (c + 1) * FF_CHUNK)
        g = jnp.dot(hb, wg_ref[:, sl], preferred_element_type=F32)
        u = jnp.dot(hb, wu_ref[:, sl], preferred_element_type=F32)
        a_ref[:, sl] = (g * jax.nn.sigmoid(g) * u).astype(BF16)
    y = x + 0.5 * jnp.dot(a_ref[...], wd_ref[...], preferred_element_type=F32)
    if final:
        y = _rms(y, fg_ref[...])
    o_ref[...] = y


def _const_spec(shape):
    return pl.BlockSpec(shape, lambda *_: (0,) * len(shape), pipeline_mode=pl.Buffered(1))


def _ffn_call(x, g, wg, wu, wd, attn=None, final_g=None):
    t = x.shape[0]
    row = lambda w: pl.BlockSpec((ROW_TILE, w), lambda i: (i, 0))
    args, specs = [x], [row(D_MODEL)]
    if attn is not None:
        o_mla, o_gqa, w_out = attn
        args += [o_mla, o_gqa, w_out]
        specs += [row(o_mla.shape[1]), row(o_gqa.shape[1]), _const_spec(w_out.shape)]
    args += [g, wg, wu, wd]
    specs += [_const_spec(a.shape) for a in (g, wg, wu, wd)]
    if final_g is not None:
        args.append(final_g)
        specs.append(_const_spec(final_g.shape))
    return pl.pallas_call(
        functools.partial(_ffn_kernel, has_attn=attn is not None, final=final_g is not None),
        grid=(t // ROW_TILE,),
        in_specs=specs,
        out_specs=row(D_MODEL),
        out_shape=jax.ShapeDtypeStruct((t, D_MODEL), F32),
        scratch_shapes=[pltpu.VMEM((ROW_TILE, D_FF), BF16)],
        compiler_params=pltpu.CompilerParams(vmem_limit_bytes=VMEM_LIMIT_BYTES),
        name="ffn_attn" if attn is not None else "ffn",
    )(*args)


def _prep_kernel(x_ref, tab_ref, gmix_ref, win_ref, gqa_ref, wqb_ref, gkv_ref, wkvb_ref, gq_ref, gk_ref,
                 gmat_ref, qm_ref, km_ref, vm_ref, qg_ref, kg_ref, vg_ref):
    hb = _rms(x_ref[...], gmix_ref[...]).astype(BF16)
    z = jnp.dot(hb, win_ref[...], preferred_element_type=F32)

    mc, msa, msb = (tab_ref[:, j * LANES:(j + 1) * LANES] for j in range(3))
    gc, gsa, gsb = (tab_ref[:, j * LANES:(j + 1) * LANES] for j in range(3, 6))
    lane = lax.broadcasted_iota(jnp.int32, (1, LANES), 1)
    lo = lane < GQA_HD
    one = (lane == MLA_V).astype(F32)

    cq = _rms(z[:, _Z_CQ[0]:_Z_CQ[1]], gqa_ref[...]).astype(BF16)
    qa = jnp.dot(cq, wqb_ref[...], preferred_element_type=F32)
    q_scale = (MLA_NOPE + MLA_ROPE) ** -0.5 * LOG2E
    for h in range(MLA_HEADS):
        sl = slice(h * LANES, (h + 1) * LANES)
        qm_ref[:, sl] = (_rope(qa[:, sl], mc, msa, msb, MLA_ROPE // 4) * q_scale).astype(BF16)

    ckv = _rms(z[:, _Z_CKV[0]:_Z_CKV[1]], gkv_ref[...]).astype(BF16)
    kv = jnp.dot(ckv, wkvb_ref[...], preferred_element_type=F32)
    kpe = _rope(z[:, _Z_KPE[0]:_Z_KPE[1]], mc, msa, msb, MLA_ROPE // 4)
    for h in range(MLA_HEADS):
        sl = slice(h * LANES, (h + 1) * LANES)
        km_ref[:, sl] = (kv[:, sl] + kpe).astype(BF16)
        vsl = slice((MLA_HEADS + h) * LANES, (MLA_HEADS + h + 1) * LANES)
        vm_ref[:, sl] = (kv[:, vsl] + one).astype(BF16)

    def head_mean_sq(v, gmat):
        sq = v * v
        hi = sq.astype(BF16)
        lo_part = (sq - hi.astype(F32)).astype(BF16)
        return (jnp.dot(hi, gmat, preferred_element_type=F32)
                + jnp.dot(lo_part, gmat, preferred_element_type=F32))

    qg = z[:, _Z_QG[0]:_Z_QG[1]]
    qg = qg * lax.rsqrt(head_mean_sq(qg, gmat_ref[...]) + EPS) * gq_ref[...]
    for j in range(GQA_HEADS // 2):
        sl = slice(j * LANES, (j + 1) * LANES)
        qg_ref[:, sl] = _rope(qg[:, sl], gc, gsa, gsb, GQA_HD // 4).astype(BF16)

    kg = z[:, _Z_KG[0]:_Z_KG[1]]
    kg = kg * lax.rsqrt(head_mean_sq(kg, gmat_ref[:LANES, :LANES]) + EPS) * gk_ref[...]
    kg = _rope(kg, gc, gsa, gsb, GQA_HD // 4)
    kg_sw = pltpu.roll(kg, GQA_HD, 1)
    zero = jnp.zeros_like(kg)
    kg_ref[:, 0 * LANES:1 * LANES] = jnp.where(lo, kg, zero).astype(BF16)
    kg_ref[:, 1 * LANES:2 * LANES] = jnp.where(lo, zero, kg_sw).astype(BF16)
    kg_ref[:, 2 * LANES:3 * LANES] = jnp.where(lo, kg_sw, zero).astype(BF16)
    kg_ref[:, 3 * LANES:4 * LANES] = jnp.where(lo, zero, kg).astype(BF16)

    vg = z[:, _Z_VG[0]:_Z_VG[1]]
    vg_sw = pltpu.roll(vg, GQA_HD, 1)
    vg_ref[:, 0 * LANES:1 * LANES] = jnp.where(lo, vg, one).astype(BF16)
    vg_ref[:, 1 * LANES:2 * LANES] = jnp.where(lo, vg_sw, one).astype(BF16)


def _prep_call(x, tab, seq, p):
    t = x.shape[0]
    tiles_per_seq = seq // ROW_TILE
    row = lambda w: pl.BlockSpec((ROW_TILE, w), lambda i: (i, 0))
    consts = [p["gmix"], p["win"], p["gqa"], p["wqb"], p["gkv"], p["wkvb"], p["gq"], p["gk"], p["gmat"]]
    widths = (MLA_HEADS * LANES,) * 3 + (GQA_HEADS // 2 * LANES, 2 * GQA_KV_HEADS * LANES, GQA_KV_HEADS * LANES)
    return pl.pallas_call(
        _prep_kernel,
        grid=(t // ROW_TILE,),
        in_specs=[row(D_MODEL), pl.BlockSpec((ROW_TILE, tab.shape[1]), lambda i: (i % tiles_per_seq, 0))]
        + [_const_spec(c.shape) for c in consts],
        out_specs=[row(w) for w in widths],
        out_shape=[jax.ShapeDtypeStruct((t, w), BF16) for w in widths],
        compiler_params=pltpu.CompilerParams(vmem_limit_bytes=VMEM_LIMIT_BYTES),
        name="prep",
    )(x, tab, *consts)


def _attn_kernel(q_ref, k_ref, v_ref, o_ref, vt_ref, sa_ref, sb_ref, acc_ref, *, q_slots, k_slots, v_slots):
    seq = k_ref.shape[0]
    nq, nc, npairs = seq // Q_TILE, seq // KEY_CHUNK, len(q_slots) // 2
    tiles_per_iter = 2 if npairs % 2 else 1
    items_per_iter = npairs * tiles_per_iter
    n_iter = nq // tiles_per_iter
    assert len(q_slots) % 2 == 0 and nq % tiles_per_iter == 0
    s_bufs = (sa_ref, sb_ref)
    neg_inf = [jnp.full((MAX_ROWS, Q_TILE), -jnp.inf, F32)] * 2

    def item(it, e):
        return it * tiles_per_iter + e // npairs, e % npairs, s_bufs[e % 2]

    def score_chunk(tile, pair, s_ref, c, mparts, heads=(0, 1)):
        qrows = pl.ds(pl.multiple_of(tile * Q_TILE, Q_TILE), Q_TILE)
        krows = pl.ds(pl.multiple_of(c * KEY_CHUNK, KEY_CHUNK), KEY_CHUNK)
        out = []
        for e in heads:
            j = 2 * pair + e
            q = q_ref[qrows, q_slots[j] * LANES:(q_slots[j] + 1) * LANES]
            k = k_ref[krows, k_slots[j] * LANES:(k_slots[j] + 1) * LANES]
            st = lax.dot_general(k, q, (((1,), (1,)), ((), ())), preferred_element_type=F32)
            s_ref[e, krows, :] = st
            part = st
            while part.shape[0] > MAX_ROWS:
                half_rows = part.shape[0] // 2
                part = jnp.maximum(part[:half_rows], part[half_rows:])
            out.append(jnp.maximum(mparts[e], part))
        return out

    def pv_chunk(pair, s_ref, c, m, heads=(0, 1)):
        krows = pl.ds(pl.multiple_of(c * KEY_CHUNK, KEY_CHUNK), KEY_CHUNK)
        for e in heads:
            pt = jnp.exp2((s_ref[e, krows, :] - m[e]).astype(BF16))
            acc_ref[e] += jnp.dot(vt_ref[v_slots[2 * pair + e], c], pt, preferred_element_type=F32)

    def col_max(mparts):
        return [jnp.max(mp, axis=0, keepdims=True) for mp in mparts]

    def finish(tile, pair):
        halves = [acc_ref[e, :MLA_V, :] / acc_ref[e, MLA_V:MLA_V + 1, :] for e in range(2)]
        ot = jnp.concatenate(halves, axis=0)
        rows = pl.ds(pl.multiple_of(tile * Q_TILE, Q_TILE), Q_TILE)
        o_ref[rows, pair * LANES:(pair + 1) * LANES] = ot.T.astype(BF16)
        acc_ref[...] = jnp.zeros_like(acc_ref)

    def item_step(it, e, m_prev):
        tile, pair, s_cur = item(it, e)
        tile_p, pair_p, s_prev = item(it, e - 1) if e > 0 else item(it - 1, items_per_iter - 1)

        def chunk(c, mp):
            out = []
            for e in range(2):
                out.append(score_chunk(tile, pair, s_cur, c, mp, heads=(e,))[0])
                pv_chunk(pair_p, s_prev, c, m_prev, heads=(e,))
            return out

        mparts = lax.fori_loop(0, nc, chunk, neg_inf, unroll=CHUNK_UNROLL)
        finish(tile_p, pair_p)
        return col_max(mparts)

    def trip(it, m, first):
        for e in range(first, items_per_iter):
            m = item_step(it, e, m)
        return m

    acc_ref[...] = jnp.zeros_like(acc_ref)
    tile0, pair0, s0 = item(0, 0)

    def first_chunk(c, mp):
        krows = pl.ds(pl.multiple_of(c * KEY_CHUNK, KEY_CHUNK), KEY_CHUNK)
        for slot in sorted(set(v_slots)):
            vt_ref[slot, c] = v_ref[krows, slot * LANES:(slot + 1) * LANES].T[:PV_ROWS, :]
        return score_chunk(tile0, pair0, s0, c, mp)

    m = col_max(lax.fori_loop(0, nc, first_chunk, neg_inf, unroll=CHUNK_UNROLL))
    m = trip(0, m, 1)
    m = lax.fori_loop(1, n_iter, lambda it, m: trip(it, m, 0), m)

    tile_l, pair_l, s_l = item(n_iter - 1, items_per_iter - 1)

    def last_chunk(c, carry):
        pv_chunk(pair_l, s_l, c, m)
        return carry

    lax.fori_loop(0, nc, last_chunk, 0, unroll=CHUNK_UNROLL)
    finish(tile_l, pair_l)


def _attn_call(q, k, v, batch, seq, q_slots, k_slots, v_slots, name):
    nh = len(q_slots)
    qw, kw, vw = ((max(s) + 1) * LANES for s in (q_slots, k_slots, v_slots))
    ow = nh // 2 * LANES
    steps = q.shape[1] // qw
    assert k.shape[1] == steps * kw and v.shape[1] == steps * vw
    return pl.pallas_call(
        functools.partial(_attn_kernel, q_slots=q_slots, k_slots=k_slots, v_slots=v_slots),
        grid=(batch, steps),
        in_specs=[
            pl.BlockSpec((seq, qw), lambda b, p: (b, p)),
            pl.BlockSpec((seq, kw), lambda b, p: (b, p)),
            pl.BlockSpec((seq, vw), lambda b, p: (b, p)),
        ],
        out_specs=pl.BlockSpec((seq, ow), lambda b, p: (b, p)),
        out_shape=jax.ShapeDtypeStruct((q.shape[0], steps * ow), BF16),
        scratch_shapes=[
            pltpu.VMEM((vw // LANES, seq // KEY_CHUNK, PV_ROWS, KEY_CHUNK), BF16),
            pltpu.VMEM((2, seq, Q_TILE), F32),
            pltpu.VMEM((2, seq, Q_TILE), F32),
            pltpu.VMEM((2, PV_ROWS, Q_TILE), F32),
        ],
        compiler_params=pltpu.CompilerParams(vmem_limit_bytes=VMEM_LIMIT_BYTES),
        name=name,
    )(q, k, v)


def _rope_tables(seq):
    rows = seq // GRID_W
    row = jnp.repeat(jnp.arange(rows, dtype=F32), GRID_W)
    col = jnp.tile(jnp.arange(GRID_W, dtype=F32), rows)

    def slot_tables(dim, lead, slot_w):
        half = dim // 2
        inv = ROPE_THETA ** (-jnp.arange(0, half, 2, dtype=F32) / half)
        ar, ac = row[:, None] * inv[None, :], col[:, None] * inv[None, :]
        z = jnp.zeros_like(ar)
        c = jnp.concatenate([jnp.cos(ar)] * 2 + [jnp.cos(ac)] * 2, axis=1)
        sa = jnp.concatenate([-jnp.sin(ar), z, -jnp.sin(ac), z], axis=1)
        sb = jnp.concatenate([z, jnp.sin(ar), z, jnp.sin(ac)], axis=1)
        tail = slot_w - lead - dim
        pad = lambda a, fill: jnp.concatenate(
            [jnp.full((seq, lead), fill, F32), a, jnp.full((seq, tail), fill, F32)], axis=1)
        return pad(c, 1.0), pad(sa, 0.0), pad(sb, 0.0)

    mla = slot_tables(MLA_ROPE, MLA_NOPE, LANES)
    gqa = [jnp.tile(a, (1, LANES // GQA_HD)) for a in slot_tables(GQA_HD, 0, GQA_HD)]
    return jnp.concatenate(list(mla) + gqa, axis=1)


def _layer_params(l, norm_ffn1, w_ffn1_gate, w_ffn1_up, w_ffn1_down, norm_mix, w_in, q_a_norm, w_q_b,
                  kv_a_norm, w_kv_b, gqa_q_norm, gqa_k_norm, w_out, norm_ffn2, w_ffn2_gate, w_ffn2_up,
                  w_ffn2_down):
    row = lambda v: v.reshape(1, -1).astype(F32)
    wi = w_in[l]
    o = np.cumsum([0, Q_LORA, KV_LORA, MLA_ROPE, GQA_HEADS * GQA_HD, GQA_KV_HEADS * GQA_HD, GQA_KV_HEADS * GQA_HD])
    zc = lambda n: jnp.zeros((D_MODEL, n), F32)
    win = jnp.concatenate(
        [wi[:, o[0]:o[1]], wi[:, o[1]:o[2]], zc(MLA_NOPE), wi[:, o[2]:o[3]], zc(LANES - MLA_NOPE - MLA_ROPE),
         wi[:, o[3]:o[4]], wi[:, o[4]:o[5]], wi[:, o[5]:o[6]]], axis=1).astype(BF16)
    assert win.shape[1] == _ZW

    wqb = w_q_b[l].reshape(Q_LORA, MLA_HEADS, MLA_NOPE + MLA_ROPE)
    wqb = jnp.pad(wqb, ((0, 0), (0, 0), (0, LANES - MLA_NOPE - MLA_ROPE))).reshape(Q_LORA, MLA_HEADS * LANES)

    wkv = w_kv_b[l].reshape(KV_LORA, MLA_HEADS, MLA_NOPE + MLA_V)
    wk = jnp.pad(wkv[:, :, :MLA_NOPE], ((0, 0), (0, 0), (0, LANES - MLA_NOPE)))
    wv = jnp.pad(wkv[:, :, MLA_NOPE:], ((0, 0), (0, 0), (0, LANES - MLA_V)))
    wkvb = jnp.concatenate([wk, wv], axis=1).reshape(KV_LORA, 2 * MLA_HEADS * LANES)

    gmat = np.kron(np.eye(GQA_HEADS), np.full((GQA_HD, GQA_HD), 1.0 / GQA_HD))
    return dict(
        g1=row(norm_ffn1[l]), wg1=w_ffn1_gate[l].astype(BF16), wu1=w_ffn1_up[l].astype(BF16),
        wd1=w_ffn1_down[l].astype(BF16),
        gmix=row(norm_mix[l]), win=win, gqa=row(q_a_norm[l]), wqb=wqb.astype(BF16), gkv=row(kv_a_norm[l]),
        wkvb=wkvb.astype(BF16),
        gq=row(jnp.tile(gqa_q_norm[l], GQA_HEADS)) * (GQA_HD ** -0.5 * LOG2E),
        gk=row(jnp.tile(gqa_k_norm[l], GQA_KV_HEADS)),
        gmat=jnp.asarray(gmat, BF16), wout=w_out[l].astype(BF16),
        g2=row(norm_ffn2[l]), wg2=w_ffn2_gate[l].astype(BF16), wu2=w_ffn2_up[l].astype(BF16),
        wd2=w_ffn2_down[l].astype(BF16),
    )


def _trunk(x3, layers, tab, final_g):
    batch, seq, _ = x3.shape
    assert seq % ROW_TILE == 0 and seq % Q_TILE == 0 and seq % GRID_W == 0
    x = x3.reshape(batch * seq, D_MODEL)
    for li, p in enumerate(layers):
        x = _ffn_call(x, p["g1"], p["wg1"], p["wu1"], p["wd1"])
        qm, km, vm, qg, kg, vg = _prep_call(x, tab, seq, p)
        o_mla = _attn_call(qm, km, vm, batch, seq, (0, 1), (0, 1), (0, 1), "attn_mla")
        o_gqa = _attn_call(qg, kg, vg, batch, seq, (0, 0, 1, 1), (0, 1, 0, 1), (0, 0, 0, 0), "attn_gqa")
        x = _ffn_call(x, p["g2"], p["wg2"], p["wu2"], p["wd2"], attn=(o_mla, o_gqa, p["wout"]),
                      final_g=final_g if li == len(layers) - 1 else None)
    return x.reshape(batch, seq, D_MODEL)


def kernel(x_prompt, x_sample, norm_ffn1, w_ffn1_gate, w_ffn1_up, w_ffn1_down, norm_mix, w_in, q_a_norm, w_q_b, kv_a_norm, w_kv_b, gqa_q_norm, gqa_k_norm, w_out, norm_ffn2, w_ffn2_gate, w_ffn2_up, w_ffn2_down, final_norm):
    depth = norm_ffn1.shape[0]
    layers = [
        _layer_params(l, norm_ffn1, w_ffn1_gate, w_ffn1_up, w_ffn1_down, norm_mix, w_in, q_a_norm, w_q_b,
                      kv_a_norm, w_kv_b, gqa_q_norm, gqa_k_norm, w_out, norm_ffn2, w_ffn2_gate, w_ffn2_up,
                      w_ffn2_down)
        for l in range(depth)
    ]
    final_g = final_norm.reshape(1, -1).astype(F32)
    outs = []
    for x3 in (x_prompt, x_sample):
        tab = _rope_tables(x3.shape[1])
        outs.append(_trunk(x3, layers, tab, final_g))
    return tuple(outs)
```

```python
import functools

import jax
import jax.numpy as jnp
import numpy as np
from jax import lax
from jax.experimental import pallas as pl
from jax.experimental.pallas import tpu as pltpu

D_MODEL = 1024
GRID_W = 64
ROPE_THETA = 10000.0
EPS = 1e-6

MLA_HEADS = 8
MLA_NOPE = 64
MLA_ROPE = 32
MLA_V = 64
Q_LORA = 256
KV_LORA = 128
GQA_HEADS = 8
GQA_KV_HEADS = 2
GQA_HD = 64
D_FF = 2816

LANES = 128
LOG2E = 1.4426950408889634

ROW_TILE = 1024
FF_CHUNK = 256
Q_TILE = 512
KEY_CHUNK = 512
PV_ROWS = 128
MAX_ROWS = 16
CHUNK_UNROLL = 4
VMEM_LIMIT_BYTES = 56 * 1024 * 1024

_ZW = Q_LORA + KV_LORA + LANES + GQA_HEADS * GQA_HD + 2 * LANES
_Z_CQ = (0, Q_LORA)
_Z_CKV = (Q_LORA, Q_LORA + KV_LORA)
_Z_KPE = (_Z_CKV[1], _Z_CKV[1] + LANES)
_Z_QG = (_Z_KPE[1], _Z_KPE[1] + GQA_HEADS * GQA_HD)
_Z_KG = (_Z_QG[1], _Z_QG[1] + LANES)
_Z_VG = (_Z_KG[1], _Z_KG[1] + LANES)

BF16 = jnp.bfloat16
F32 = jnp.float32


def _rms(x, g):
    ms = jnp.mean(x * x, axis=-1, keepdims=True)
    return x * lax.rsqrt(ms + EPS) * g


def _rope(x, c, sa, sb, d):
    return x * c + pltpu.roll(x, LANES - d, 1) * sa + pltpu.roll(x, d, 1) * sb


def _ffn_kernel(*refs, has_attn, final):
    it = iter(refs)
    x_ref = next(it)
    if has_attn:
        om_ref, og_ref, wo_ref = next(it), next(it), next(it)
    g_ref, wg_ref, wu_ref, wd_ref = next(it), next(it), next(it), next(it)
    if final:
        fg_ref = next(it)
    o_ref = next(it)
    a_ref = next(it)

    x = x_ref[...]
    if has_attn:
        half = om_ref.shape[0]
        tn = (((0,), (0,)), ((), ()))
        x = x + lax.dot_general(om_ref[...], wo_ref[:half, :], tn, preferred_element_type=F32)
        x = x + lax.dot_general(og_ref[...], wo_ref[half:, :], tn, preferred_element_type=F32)
    hb = _rms(x, g_ref[...]).astype(BF16)
    for c in range(D_FF // FF_CHUNK):
        sl = slice(c * FF_CHUNK, (c + 1) * FF_CHUNK)
        g = jnp.dot(hb, wg_ref[:, sl], preferred_element_type=F32)
        u = jnp.dot(hb, wu_ref[:, sl], preferred_element_type=F32)
        a_ref[:, sl] = (g * jax.nn.sigmoid(g) * u).astype(BF16)
    y = x + 0.5 * jnp.dot(a_ref[...], wd_ref[...], preferred_element_type=F32)
    if final:
        y = _rms(y, fg_ref[...])
    o_ref[...] = y


def _const_spec(shape):
    return pl.BlockSpec(shape, lambda *_: (0,) * len(shape), pipeline_mode=pl.Buffered(1))


def _ffn_call(x, g, wg, wu, wd, attn=None, final_g=None):
    t = x.shape[0]
    row = lambda w: pl.BlockSpec((ROW_TILE, w), lambda i: (i, 0))
    args, specs = [x], [row(D_MODEL)]
    if attn is not None:
        o_mla, o_gqa, w_out = attn
        args += [o_mla, o_gqa, w_out]
        col = lambda a: pl.BlockSpec((a.shape[0], ROW_TILE), lambda i: (0, i))
        specs += [col(o_mla), col(o_gqa), _const_spec(w_out.shape)]
    args += [g, wg, wu, wd]
    specs += [_const_spec(a.shape) for a in (g, wg, wu, wd)]
    if final_g is not None:
        args.append(final_g)
        specs.append(_const_spec(final_g.shape))
    return pl.pallas_call(
        functools.partial(_ffn_kernel, has_attn=attn is not None, final=final_g is not None),
        grid=(t // ROW_TILE,),
        in_specs=specs,
        out_specs=row(D_MODEL),
        out_shape=jax.ShapeDtypeStruct((t, D_MODEL), F32),
        scratch_shapes=[pltpu.VMEM((ROW_TILE, D_FF), BF16)],
        compiler_params=pltpu.CompilerParams(vmem_limit_bytes=VMEM_LIMIT_BYTES),
        name="ffn_attn" if attn is not None else "ffn",
    )(*args)


def _prep_kernel(x_ref, tab_ref, gmix_ref, win_ref, gqa_ref, wqb_ref, gkv_ref, wkvb_ref, gq_ref, gk_ref,
                 gmat_ref, qm_ref, km_ref, vm_ref, qg_ref, kg_ref, vg_ref):
    hb = _rms(x_ref[...], gmix_ref[...]).astype(BF16)
    z = jnp.dot(hb, win_ref[...], preferred_element_type=F32)

    mc, msa, msb = (tab_ref[:, j * LANES:(j + 1) * LANES] for j in range(3))
    gc, gsa, gsb = (tab_ref[:, j * LANES:(j + 1) * LANES] for j in range(3, 6))
    lane = lax.broadcasted_iota(jnp.int32, (1, LANES), 1)
    lo = lane < GQA_HD
    one = (lane == MLA_V).astype(F32)

    cq = _rms(z[:, _Z_CQ[0]:_Z_CQ[1]], gqa_ref[...]).astype(BF16)
    qa = jnp.dot(cq, wqb_ref[...], preferred_element_type=F32)
    q_scale = (MLA_NOPE + MLA_ROPE) ** -0.5 * LOG2E
    for h in range(MLA_HEADS):
        sl = slice(h * LANES, (h + 1) * LANES)
        qm_ref[:, sl] = (_rope(qa[:, sl], mc, msa, msb, MLA_ROPE // 4) * q_scale).astype(BF16)

    ckv = _rms(z[:, _Z_CKV[0]:_Z_CKV[1]], gkv_ref[...]).astype(BF16)
    kv = jnp.dot(ckv, wkvb_ref[...], preferred_element_type=F32)
    kpe = _rope(z[:, _Z_KPE[0]:_Z_KPE[1]], mc, msa, msb, MLA_ROPE // 4)
    for h in range(MLA_HEADS):
        sl = slice(h * LANES, (h + 1) * LANES)
        km_ref[:, sl] = (kv[:, sl] + kpe).astype(BF16)
        vsl = slice((MLA_HEADS + h) * LANES, (MLA_HEADS + h + 1) * LANES)
        vm_ref[:, sl] = (kv[:, vsl] + one).astype(BF16)

    def head_mean_sq(v, gmat):
        sq = v * v
        hi = sq.astype(BF16)
        lo_part = (sq - hi.astype(F32)).astype(BF16)
        return (jnp.dot(hi, gmat, preferred_element_type=F32)
                + jnp.dot(lo_part, gmat, preferred_element_type=F32))

    qg = z[:, _Z_QG[0]:_Z_QG[1]]
    qg = qg * lax.rsqrt(head_mean_sq(qg, gmat_ref[...]) + EPS) * gq_ref[...]
    for j in range(GQA_HEADS // 2):
        sl = slice(j * LANES, (j + 1) * LANES)
        qg_ref[:, sl] = _rope(qg[:, sl], gc, gsa, gsb, GQA_HD // 4).astype(BF16)

    kg = z[:, _Z_KG[0]:_Z_KG[1]]
    kg = kg * lax.rsqrt(head_mean_sq(kg, gmat_ref[:LANES, :LANES]) + EPS) * gk_ref[...]
    kg = _rope(kg, gc, gsa, gsb, GQA_HD // 4)
    kg_sw = pltpu.roll(kg, GQA_HD, 1)
    zero = jnp.zeros_like(kg)
    kg_ref[:, 0 * LANES:1 * LANES] = jnp.where(lo, kg, zero).astype(BF16)
    kg_ref[:, 1 * LANES:2 * LANES] = jnp.where(lo, zero, kg_sw).astype(BF16)
    kg_ref[:, 2 * LANES:3 * LANES] = jnp.where(lo, kg_sw, zero).astype(BF16)
    kg_ref[:, 3 * LANES:4 * LANES] = jnp.where(lo, zero, kg).astype(BF16)

    vg = z[:, _Z_VG[0]:_Z_VG[1]]
    vg_sw = pltpu.roll(vg, GQA_HD, 1)
    vg_ref[:, 0 * LANES:1 * LANES] = jnp.where(lo, vg, one).astype(BF16)
    vg_ref[:, 1 * LANES:2 * LANES] = jnp.where(lo, vg_sw, one).astype(BF16)


def _prep_call(x, tab, seq, p):
    t = x.shape[0]
    tiles_per_seq = seq // ROW_TILE
    row = lambda w: pl.BlockSpec((ROW_TILE, w), lambda i: (i, 0))
    consts = [p["gmix"], p["win"], p["gqa"], p["wqb"], p["gkv"], p["wkvb"], p["gq"], p["gk"], p["gmat"]]
    widths = (MLA_HEADS * LANES,) * 3 + (GQA_HEADS // 2 * LANES, 2 * GQA_KV_HEADS * LANES, GQA_KV_HEADS * LANES)
    return pl.pallas_call(
        _prep_kernel,
        grid=(t // ROW_TILE,),
        in_specs=[row(D_MODEL), pl.BlockSpec((ROW_TILE, tab.shape[1]), lambda i: (i % tiles_per_seq, 0))]
        + [_const_spec(c.shape) for c in consts],
        out_specs=[row(w) for w in widths],
        out_shape=[jax.ShapeDtypeStruct((t, w), BF16) for w in widths],
        compiler_params=pltpu.CompilerParams(vmem_limit_bytes=VMEM_LIMIT_BYTES),
        name="prep",
    )(x, tab, *consts)


def _attn_kernel(q_ref, k_ref, v_ref, o_ref, vt_ref, sa_ref, sb_ref, acc_ref, *, q_slots, k_slots, v_slots):
    seq = k_ref.shape[0]
    nq, nc, npairs = seq // Q_TILE, seq // KEY_CHUNK, len(q_slots) // 2
    tiles_per_iter = 2 if npairs % 2 else 1
    items_per_iter = npairs * tiles_per_iter
    n_iter = nq // tiles_per_iter
    assert len(q_slots) % 2 == 0 and nq % tiles_per_iter == 0
    s_bufs = (sa_ref, sb_ref)
    neg_inf = [jnp.full((MAX_ROWS, Q_TILE), -jnp.inf, F32)] * 2

    def item(it, e):
        return it * tiles_per_iter + e // npairs, e % npairs, s_bufs[e % 2]

    def score_chunk(tile, pair, s_ref, c, mparts, heads=(0, 1)):
        qrows = pl.ds(pl.multiple_of(tile * Q_TILE, Q_TILE), Q_TILE)
        krows = pl.ds(pl.multiple_of(c * KEY_CHUNK, KEY_CHUNK), KEY_CHUNK)
        out = []
        for e in heads:
            j = 2 * pair + e
            q = q_ref[qrows, q_slots[j] * LANES:(q_slots[j] + 1) * LANES]
            k = k_ref[krows, k_slots[j] * LANES:(k_slots[j] + 1) * LANES]
            st = lax.dot_general(k, q, (((1,), (1,)), ((), ())), preferred_element_type=F32)
            s_ref[e, krows, :] = st
            part = jnp.max(st.reshape(KEY_CHUNK // MAX_ROWS, MAX_ROWS, Q_TILE), axis=0)
            out.append(jnp.maximum(mparts[e], part))
        return out

    def pv_chunk(pair, s_ref, c, m, heads=(0, 1)):
        krows = pl.ds(pl.multiple_of(c * KEY_CHUNK, KEY_CHUNK), KEY_CHUNK)
        for e in heads:
            pt = jnp.exp2((s_ref[e, krows, :] - m[e]).astype(BF16))
            acc_ref[e] += jnp.dot(vt_ref[v_slots[2 * pair + e], c], pt, preferred_element_type=F32)

    def col_max(mparts):
        return [jnp.max(mp, axis=0, keepdims=True) for mp in mparts]

    def finish(tile, pair):
        halves = [acc_ref[e, :MLA_V, :] / acc_ref[e, MLA_V:MLA_V + 1, :] for e in range(2)]
        ot = jnp.concatenate(halves, axis=0)
        cols = pl.ds(pl.multiple_of(tile * Q_TILE, Q_TILE), Q_TILE)
        o_ref[pair * LANES:(pair + 1) * LANES, cols] = ot.astype(BF16)
        acc_ref[...] = jnp.zeros_like(acc_ref)

    def item_step(it, e, m_prev):
        tile, pair, s_cur = item(it, e)
        tile_p, pair_p, s_prev = item(it, e - 1) if e > 0 else item(it - 1, items_per_iter - 1)

        def chunk(c, mp):
            out = []
            for e in range(2):
                out.append(score_chunk(tile, pair, s_cur, c, mp, heads=(e,))[0])
                pv_chunk(pair_p, s_prev, c, m_prev, heads=(e,))
            return out

        mparts = lax.fori_loop(0, nc, chunk, neg_inf, unroll=CHUNK_UNROLL)
        finish(tile_p, pair_p)
        return col_max(mparts)

    def trip(it, m, first):
        for e in range(first, items_per_iter):
            m = item_step(it, e, m)
        return m

    acc_ref[...] = jnp.zeros_like(acc_ref)
    tile0, pair0, s0 = item(0, 0)

    def first_chunk(c, mp):
        krows = pl.ds(pl.multiple_of(c * KEY_CHUNK, KEY_CHUNK), KEY_CHUNK)
        for slot in sorted(set(v_slots)):
            vt_ref[slot, c] = v_ref[krows, slot * LANES:(slot + 1) * LANES].T[:PV_ROWS, :]
        return score_chunk(tile0, pair0, s0, c, mp)

    m = col_max(lax.fori_loop(0, nc, first_chunk, neg_inf, unroll=CHUNK_UNROLL))
    m = trip(0, m, 1)
    m = lax.fori_loop(1, n_iter, lambda it, m: trip(it, m, 0), m)

    tile_l, pair_l, s_l = item(n_iter - 1, items_per_iter - 1)

    def last_chunk(c, carry):
        pv_chunk(pair_l, s_l, c, m)
        return carry

    lax.fori_loop(0, nc, last_chunk, 0, unroll=CHUNK_UNROLL)
    finish(tile_l, pair_l)


def _attn_call(q, k, v, batch, seq, q_slots, k_slots, v_slots, name):
    nh = len(q_slots)
    qw, kw, vw = ((max(s) + 1) * LANES for s in (q_slots, k_slots, v_slots))
    ow = nh // 2 * LANES
    steps = q.shape[1] // qw
    assert k.shape[1] == steps * kw and v.shape[1] == steps * vw
    return pl.pallas_call(
        functools.partial(_attn_kernel, q_slots=q_slots, k_slots=k_slots, v_slots=v_slots),
        grid=(batch, steps),
        in_specs=[
            pl.BlockSpec((seq, qw), lambda b, p: (b, p)),
            pl.BlockSpec((seq, kw), lambda b, p: (b, p)),
            pl.BlockSpec((seq, vw), lambda b, p: (b, p)),
        ],
        out_specs=pl.BlockSpec((ow, seq), lambda b, p: (p, b)),
        out_shape=jax.ShapeDtypeStruct((steps * ow, q.shape[0]), BF16),
        scratch_shapes=[
            pltpu.VMEM((vw // LANES, seq // KEY_CHUNK, PV_ROWS, KEY_CHUNK), BF16),
            pltpu.VMEM((2, seq, Q_TILE), F32),
            pltpu.VMEM((2, seq, Q_TILE), F32),
            pltpu.VMEM((2, PV_ROWS, Q_TILE), F32),
        ],
        compiler_params=pltpu.CompilerParams(vmem_limit_bytes=VMEM_LIMIT_BYTES),
        name=name,
    )(q, k, v)


def _rope_tables(seq):
    rows = seq // GRID_W
    row = jnp.repeat(jnp.arange(rows, dtype=F32), GRID_W)
    col = jnp.tile(jnp.arange(GRID_W, dtype=F32), rows)

    def slot_tables(dim, lead, slot_w):
        half = dim // 2
        inv = ROPE_THETA ** (-jnp.arange(0, half, 2, dtype=F32) / half)
        ar, ac = row[:, None] * inv[None, :], col[:, None] * inv[None, :]
        z = jnp.zeros_like(ar)
        c = jnp.concatenate([jnp.cos(ar)] * 2 + [jnp.cos(ac)] * 2, axis=1)
        sa = jnp.concatenate([-jnp.sin(ar), z, -jnp.sin(ac), z], axis=1)
        sb = jnp.concatenate([z, jnp.sin(ar), z, jnp.sin(ac)], axis=1)
        tail = slot_w - lead - dim
        pad = lambda a, fill: jnp.concatenate(
            [jnp.full((seq, lead), fill, F32), a, jnp.full((seq, tail), fill, F32)], axis=1)
        return pad(c, 1.0), pad(sa, 0.0), pad(sb, 0.0)

    mla = slot_tables(MLA_ROPE, MLA_NOPE, LANES)
    gqa = [jnp.tile(a, (1, LANES // GQA_HD)) for a in slot_tables(GQA_HD, 0, GQA_HD)]
    return jnp.concatenate(list(mla) + gqa, axis=1)


def _layer_params(l, norm_ffn1, w_ffn1_gate, w_ffn1_up, w_ffn1_down, norm_mix, w_in, q_a_norm, w_q_b,
                  kv_a_norm, w_kv_b, gqa_q_norm, gqa_k_norm, w_out, norm_ffn2, w_ffn2_gate, w_ffn2_up,
                  w_ffn2_down):
    row = lambda v: v.reshape(1, -1).astype(F32)
    wi = w_in[l]
    o = np.cumsum([0, Q_LORA, KV_LORA, MLA_ROPE, GQA_HEADS * GQA_HD, GQA_KV_HEADS * GQA_HD, GQA_KV_HEADS * GQA_HD])
    zc = lambda n: jnp.zeros((D_MODEL, n), F32)
    win = jnp.concatenate(
        [wi[:, o[0]:o[1]], wi[:, o[1]:o[2]], zc(MLA_NOPE), wi[:, o[2]:o[3]], zc(LANES - MLA_NOPE - MLA_ROPE),
         wi[:, o[3]:o[4]], wi[:, o[4]:o[5]], wi[:, o[5]:o[6]]], axis=1).astype(BF16)
    assert win.shape[1] == _ZW

    wqb = w_q_b[l].reshape(Q_LORA, MLA_HEADS, MLA_NOPE + MLA_ROPE)
    wqb = jnp.pad(wqb, ((0, 0), (0, 0), (0, LANES - MLA_NOPE - MLA_ROPE))).reshape(Q_LORA, MLA_HEADS * LANES)

    wkv = w_kv_b[l].reshape(KV_LORA, MLA_HEADS, MLA_NOPE + MLA_V)
    wk = jnp.pad(wkv[:, :, :MLA_NOPE], ((0, 0), (0, 0), (0, LANES - MLA_NOPE)))
    wv = jnp.pad(wkv[:, :, MLA_NOPE:], ((0, 0), (0, 0), (0, LANES - MLA_V)))
    wkvb = jnp.concatenate([wk, wv], axis=1).reshape(KV_LORA, 2 * MLA_HEADS * LANES)

    gmat = np.kron(np.eye(GQA_HEADS), np.full((GQA_HD, GQA_HD), 1.0 / GQA_HD))
    return dict(
        g1=row(norm_ffn1[l]), wg1=w_ffn1_gate[l].astype(BF16), wu1=w_ffn1_up[l].astype(BF16),
        wd1=w_ffn1_down[l].astype(BF16),
        gmix=row(norm_mix[l]), win=win, gqa=row(q_a_norm[l]), wqb=wqb.astype(BF16), gkv=row(kv_a_norm[l]),
        wkvb=wkvb.astype(BF16),
        gq=row(jnp.tile(gqa_q_norm[l], GQA_HEADS)) * (GQA_HD ** -0.5 * LOG2E),
        gk=row(jnp.tile(gqa_k_norm[l], GQA_KV_HEADS)),
        gmat=jnp.asarray(gmat, BF16), wout=w_out[l].astype(BF16),
        g2=row(norm_ffn2[l]), wg2=w_ffn2_gate[l].astype(BF16), wu2=w_ffn2_up[l].astype(BF16),
        wd2=w_ffn2_down[l].astype(BF16),
    )


def _trunk(x3, layers, tab, final_g):
    batch, seq, _ = x3.shape
    assert seq % ROW_TILE == 0 and seq % Q_TILE == 0 and seq % GRID_W == 0
    x = x3.reshape(batch * seq, D_MODEL)
    for li, p in enumerate(layers):
        x = _ffn_call(x, p["g1"], p["wg1"], p["wu1"], p["wd1"])
        qm, km, vm, qg, kg, vg = _prep_call(x, tab, seq, p)
        o_mla = _attn_call(qm, km, vm, batch, seq, (0, 1), (0, 1), (0, 1), "attn_mla")
        o_gqa = _attn_call(qg, kg, vg, batch, seq, (0, 0, 1, 1), (0, 1, 0, 1), (0, 0, 0, 0), "attn_gqa")
        x = _ffn_call(x, p["g2"], p["wg2"], p["wu2"], p["wd2"], attn=(o_mla, o_gqa, p["wout"]),
                      final_g=final_g if li == len(layers) - 1 else None)
    return x.reshape(batch, seq, D_MODEL)


def kernel(x_prompt, x_sample, norm_ffn1, w_ffn1_gate, w_ffn1_up, w_ffn1_down, norm_mix, w_in, q_a_norm, w_q_b, kv_a_norm, w_kv_b, gqa_q_norm, gqa_k_norm, w_out, norm_ffn2, w_ffn2_gate, w_ffn2_up, w_ffn2_down, final_norm):
    depth = norm_ffn1.shape[0]
    layers = [
        _layer_params(l, norm_ffn1, w_ffn1_gate, w_ffn1_up, w_ffn1_down, norm_mix, w_in, q_a_norm, w_q_b,
                      kv_a_norm, w_kv_b, gqa_q_norm, gqa_k_norm, w_out, norm_ffn2, w_ffn2_gate, w_ffn2_up,
                      w_ffn2_down)
        for l in range(depth)
    ]
    final_g = final_norm.reshape(1, -1).astype(F32)
    outs = []
    for x3 in (x_prompt, x_sample):
        tab = _rope_tables(x3.shape[1])
        outs.append(_trunk(x3, layers, tab, final_g))
    return tuple(outs)
```

```python
import functools

import jax
import jax.numpy as jnp
import numpy as np
from jax import lax
from jax.experimental import pallas as pl
from jax.experimental.pallas import tpu as pltpu

D_MODEL = 1024
GRID_W = 64
ROPE_THETA = 10000.0
EPS = 1e-6

MLA_HEADS = 8
MLA_NOPE = 64
MLA_ROPE = 32
MLA_V = 64
Q_LORA = 256
KV_LORA = 128
GQA_HEADS = 8
GQA_KV_HEADS = 2
GQA_HD = 64
D_FF = 2816

LANES = 128
LOG2E = 1.4426950408889634

ROW_TILE = 1024
FF_CHUNK = 256
Q_TILE = 512
KEY_CHUNK = 512
PV_ROWS = 128
MAX_ROWS = 16
CHUNK_UNROLL = 4
VMEM_LIMIT_BYTES = 56 * 1024 * 1024

_ZW = Q_LORA + KV_LORA + LANES + GQA_HEADS * GQA_HD + 2 * LANES
_Z_CQ = (0, Q_LORA)
_Z_CKV = (Q_LORA, Q_LORA + KV_LORA)
_Z_KPE = (_Z_CKV[1], _Z_CKV[1] + LANES)
_Z_QG = (_Z_KPE[1], _Z_KPE[1] + GQA_HEADS * GQA_HD)
_Z_KG = (_Z_QG[1], _Z_QG[1] + LANES)
_Z_VG = (_Z_KG[1], _Z_KG[1] + LANES)

BF16 = jnp.bfloat16
F32 = jnp.float32


def _rms(x, g):
    ms = jnp.mean(x * x, axis=-1, keepdims=True)
    return x * lax.rsqrt(ms + EPS) * g


def _rope(x, c, sa, sb, d):
    return x * c + pltpu.roll(x, LANES - d, 1) * sa + pltpu.roll(x, d, 1) * sb


def _ffn_kernel(*refs, has_attn, final):
    it = iter(refs)
    x_ref = next(it)
    if has_attn:
        om_ref, og_ref, wo_ref = next(it), next(it), next(it)
    g_ref, wg_ref, wu_ref, wd_ref = next(it), next(it), next(it), next(it)
    if final:
        fg_ref = next(it)
    o_ref = next(it)
    a_ref = next(it)

    x = x_ref[...]
    if has_attn:
        half = om_ref.shape[0]
        tn = (((0,), (0,)), ((), ()))
        x = x + lax.dot_general(om_ref[...], wo_ref[:half, :], tn, preferred_element_type=F32)
        x = x + lax.dot_general(og_ref[...], wo_ref[half:, :], tn, preferred_element_type=F32)
    hb = _rms(x, g_ref[...]).astype(BF16)
    for c in range(D_FF // FF_CHUNK):
        sl = slice(c * FF_CHUNK, (c + 1) * FF_CHUNK)
        g = jnp.dot(hb, wg_ref[:, sl], preferred_element_type=F32)
        u = jnp.dot(hb, wu_ref[:, sl], preferred_element_type=F32)
        a_ref[:, sl] = (g * jax.nn.sigmoid(g) * u).astype(BF16)
    y = x + 0.5 * jnp.dot(a_ref[...], wd_ref[...], preferred_element_type=F32)
    if final:
        y = _rms(y, fg_ref[...])
    o_ref[...] = y


def _const_spec(shape):
    return pl.BlockSpec(shape, lambda *_: (0,) * len(shape), pipeline_mode=pl.Buffered(1))


def _ffn_call(x, g, wg, wu, wd, attn=None, final_g=None):
    t = x.shape[0]
    row = lambda w: pl.BlockSpec((ROW_TILE, w), lambda i: (i, 0))
    args, specs = [x], [row(D_MODEL)]
    if attn is not None:
        o_mla, o_gqa, w_out = attn
        args += [o_mla, o_gqa, w_out]
        col = lambda a: pl.BlockSpec((a.shape[0], ROW_TILE), lambda i: (0, i))
        specs += [col(o_mla), col(o_gqa), _const_spec(w_out.shape)]
    args += [g, wg, wu, wd]
    specs += [_const_spec(a.shape) for a in (g, wg, wu, wd)]
    if final_g is not None:
        args.append(final_g)
        specs.append(_const_spec(final_g.shape))
    return pl.pallas_call(
        functools.partial(_ffn_kernel, has_attn=attn is not None, final=final_g is not None),
        grid=(t // ROW_TILE,),
        in_specs=specs,
        out_specs=row(D_MODEL),
        out_shape=jax.ShapeDtypeStruct((t, D_MODEL), F32),
        scratch_shapes=[pltpu.VMEM((ROW_TILE, D_FF), BF16)],
        compiler_params=pltpu.CompilerParams(vmem_limit_bytes=VMEM_LIMIT_BYTES),
        name="ffn_attn" if attn is not None else "ffn",
    )(*args)


def _prep_kernel(x_ref, tab_ref, gmix_ref, win_ref, gqa_ref, wqb_ref, gkv_ref, wkvb_ref, gq_ref, gk_ref,
                 gmat_ref, qm_ref, km_ref, vm_ref, qg_ref, kg_ref, vg_ref):
    hb = _rms(x_ref[...], gmix_ref[...]).astype(BF16)
    z = jnp.dot(hb, win_ref[...], preferred_element_type=F32)

    mc, msa, msb = (tab_ref[:, j * LANES:(j + 1) * LANES] for j in range(3))
    gc, gsa, gsb = (tab_ref[:, j * LANES:(j + 1) * LANES] for j in range(3, 6))
    lane = lax.broadcasted_iota(jnp.int32, (1, LANES), 1)
    lo = lane < GQA_HD
    one = (lane == MLA_V).astype(F32)

    cq = _rms(z[:, _Z_CQ[0]:_Z_CQ[1]], gqa_ref[...]).astype(BF16)
    qa = jnp.dot(cq, wqb_ref[...], preferred_element_type=F32)
    q_scale = (MLA_NOPE + MLA_ROPE) ** -0.5 * LOG2E
    for h in range(MLA_HEADS):
        sl = slice(h * LANES, (h + 1) * LANES)
        qm_ref[:, sl] = (_rope(qa[:, sl], mc, msa, msb, MLA_ROPE // 4) * q_scale).astype(BF16)

    ckv = _rms(z[:, _Z_CKV[0]:_Z_CKV[1]], gkv_ref[...]).astype(BF16)
    kv = jnp.dot(ckv, wkvb_ref[...], preferred_element_type=F32)
    kpe = _rope(z[:, _Z_KPE[0]:_Z_KPE[1]], mc, msa, msb, MLA_ROPE // 4)
    for h in range(MLA_HEADS):
        sl = slice(h * LANES, (h + 1) * LANES)
        km_ref[:, sl] = (kv[:, sl] + kpe).astype(BF16)
        vsl = slice((MLA_HEADS + h) * LANES, (MLA_HEADS + h + 1) * LANES)
        vm_ref[:, sl] = (kv[:, vsl] + one).astype(BF16)

    def head_mean_sq(v, gmat):
        sq = v * v
        hi = sq.astype(BF16)
        lo_part = (sq - hi.astype(F32)).astype(BF16)
        return (jnp.dot(hi, gmat, preferred_element_type=F32)
                + jnp.dot(lo_part, gmat, preferred_element_type=F32))

    qg = z[:, _Z_QG[0]:_Z_QG[1]]
    qg = qg * lax.rsqrt(head_mean_sq(qg, gmat_ref[...]) + EPS) * gq_ref[...]
    for j in range(GQA_HEADS // 2):
        sl = slice(j * LANES, (j + 1) * LANES)
        qg_ref[:, sl] = _rope(qg[:, sl], gc, gsa, gsb, GQA_HD // 4).astype(BF16)

    kg = z[:, _Z_KG[0]:_Z_KG[1]]
    kg = kg * lax.rsqrt(head_mean_sq(kg, gmat_ref[:LANES, :LANES]) + EPS) * gk_ref[...]
    kg = _rope(kg, gc, gsa, gsb, GQA_HD // 4)
    kg_sw = pltpu.roll(kg, GQA_HD, 1)
    zero = jnp.zeros_like(kg)
    kg_ref[:, 0 * LANES:1 * LANES] = jnp.where(lo, kg, zero).astype(BF16)
    kg_ref[:, 1 * LANES:2 * LANES] = jnp.where(lo, zero, kg_sw).astype(BF16)
    kg_ref[:, 2 * LANES:3 * LANES] = jnp.where(lo, kg_sw, zero).astype(BF16)
    kg_ref[:, 3 * LANES:4 * LANES] = jnp.where(lo, zero, kg).astype(BF16)

    vg = z[:, _Z_VG[0]:_Z_VG[1]]
    vg_sw = pltpu.roll(vg, GQA_HD, 1)
    vg_ref[:, 0 * LANES:1 * LANES] = jnp.where(lo, vg, one).astype(BF16)
    vg_ref[:, 1 * LANES:2 * LANES] = jnp.where(lo, vg_sw, one).astype(BF16)


def _prep_call(x, tab, seq, p):
    t = x.shape[0]
    tiles_per_seq = seq // ROW_TILE
    row = lambda w: pl.BlockSpec((ROW_TILE, w), lambda i: (i, 0))
    consts = [p["gmix"], p["win"], p["gqa"], p["wqb"], p["gkv"], p["wkvb"], p["gq"], p["gk"], p["gmat"]]
    widths = (MLA_HEADS * LANES,) * 3 + (GQA_HEADS // 2 * LANES, 2 * GQA_KV_HEADS * LANES, GQA_KV_HEADS * LANES)
    return pl.pallas_call(
        _prep_kernel,
        grid=(t // ROW_TILE,),
        in_specs=[row(D_MODEL), pl.BlockSpec((ROW_TILE, tab.shape[1]), lambda i: (i % tiles_per_seq, 0))]
        + [_const_spec(c.shape) for c in consts],
        out_specs=[row(w) for w in widths],
        out_shape=[jax.ShapeDtypeStruct((t, w), BF16) for w in widths],
        compiler_params=pltpu.CompilerParams(vmem_limit_bytes=VMEM_LIMIT_BYTES),
        name="prep",
    )(x, tab, *consts)


def _attn_kernel(q_ref, k_ref, v_ref, o_ref, vt_ref, sa_ref, sb_ref, acc_ref, *, q_slots, k_slots, v_slots):
    seq = k_ref.shape[0]
    nq, nc, npairs = seq // Q_TILE, seq // KEY_CHUNK, len(q_slots) // 2
    tiles_per_iter = 2 if npairs % 2 else 1
    items_per_iter = npairs * tiles_per_iter
    n_iter = nq // tiles_per_iter
    assert len(q_slots) % 2 == 0 and nq % tiles_per_iter == 0
    s_bufs = (sa_ref, sb_ref)
    neg_inf = [jnp.full((MAX_ROWS, Q_TILE), -jnp.inf, F32)] * 2

    def item(it, e):
        return it * tiles_per_iter + e // npairs, e % npairs, s_bufs[e % 2]

    def score_chunk(tile, pair, s_ref, c, mparts, heads=(0, 1)):
        qrows = pl.ds(pl.multiple_of(tile * Q_TILE, Q_TILE), Q_TILE)
        krows = pl.ds(pl.multiple_of(c * KEY_CHUNK, KEY_CHUNK), KEY_CHUNK)
        out = []
        for e in heads:
            j = 2 * pair + e
            q = q_ref[qrows, q_slots[j] * LANES:(q_slots[j] + 1) * LANES]
            k = k_ref[krows, k_slots[j] * LANES:(k_slots[j] + 1) * LANES]
            st = lax.dot_general(k, q, (((1,), (1,)), ((), ())), preferred_element_type=F32)
            s_ref[e, krows, :] = st
            part = jnp.max(st.reshape(KEY_CHUNK // MAX_ROWS, MAX_ROWS, Q_TILE), axis=0)
            out.append(jnp.maximum(mparts[e], part))
        return out

    def pv_chunk(pair, s_ref, c, m, heads=(0, 1)):
        krows = pl.ds(pl.multiple_of(c * KEY_CHUNK, KEY_CHUNK), KEY_CHUNK)
        for e in heads:
            pt = jnp.exp2((s_ref[e, krows, :] - m[e]).astype(BF16))
            acc_ref[e] += jnp.dot(vt_ref[v_slots[2 * pair + e], c], pt, preferred_element_type=F32)

    def col_max(mparts):
        return [jnp.max(mp, axis=0, keepdims=True) for mp in mparts]

    def finish(tile, pair):
        halves = [acc_ref[e, :MLA_V, :] / acc_ref[e, MLA_V:MLA_V + 1, :] for e in range(2)]
        ot = jnp.concatenate(halves, axis=0)
        cols = pl.ds(pl.multiple_of(tile * Q_TILE, Q_TILE), Q_TILE)
        o_ref[pair * LANES:(pair + 1) * LANES, cols] = ot.astype(BF16)
        acc_ref[...] = jnp.zeros_like(acc_ref)

    def item_step(it, e, m_prev):
        tile, pair, s_cur = item(it, e)
        tile_p, pair_p, s_prev = item(it, e - 1) if e > 0 else item(it - 1, items_per_iter - 1)

        def chunk(c, mp):
            out = []
            for e in range(2):
                out.append(score_chunk(tile, pair, s_cur, c, mp, heads=(e,))[0])
                pv_chunk(pair_p, s_prev, c, m_prev, heads=(e,))
            return out

        mparts = lax.fori_loop(0, nc, chunk, neg_inf, unroll=CHUNK_UNROLL)
        finish(tile_p, pair_p)
        return col_max(mparts)

    def trip(it, m, first):
        for e in range(first, items_per_iter):
            m = item_step(it, e, m)
        return m

    acc_ref[...] = jnp.zeros_like(acc_ref)
    tile0, pair0, s0 = item(0, 0)

    def first_chunk(c, mp):
        krows = pl.ds(pl.multiple_of(c * KEY_CHUNK, KEY_CHUNK), KEY_CHUNK)
        for slot in sorted(set(v_slots)):
            vt_ref[slot, c] = v_ref[krows, slot * LANES:(slot + 1) * LANES].T[:PV_ROWS, :]
        return score_chunk(tile0, pair0, s0, c, mp)

    m = col_max(lax.fori_loop(0, nc, first_chunk, neg_inf, unroll=CHUNK_UNROLL))
    m = trip(0, m, 1)
    m = lax.fori_loop(1, n_iter, lambda it, m: trip(it, m, 0), m)

    tile_l, pair_l, s_l = item(n_iter - 1, items_per_iter - 1)

    def last_chunk(c, carry):
        pv_chunk(pair_l, s_l, c, m)
        return carry

    lax.fori_loop(0, nc, last_chunk, 0, unroll=CHUNK_UNROLL)
    finish(tile_l, pair_l)


def _attn_call(q, k, v, batch, seq, q_slots, k_slots, v_slots, name, single_buffer_inputs=False):
    nh = len(q_slots)
    qw, kw, vw = ((max(s) + 1) * LANES for s in (q_slots, k_slots, v_slots))
    ow = nh // 2 * LANES
    steps = q.shape[1] // qw
    assert k.shape[1] == steps * kw and v.shape[1] == steps * vw
    mode = dict(pipeline_mode=pl.Buffered(1)) if single_buffer_inputs else {}
    return pl.pallas_call(
        functools.partial(_attn_kernel, q_slots=q_slots, k_slots=k_slots, v_slots=v_slots),
        grid=(batch, steps),
        in_specs=[
            pl.BlockSpec((seq, qw), lambda b, p: (b, p), **mode),
            pl.BlockSpec((seq, kw), lambda b, p: (b, p), **mode),
            pl.BlockSpec((seq, vw), lambda b, p: (b, p), **mode),
        ],
        out_specs=pl.BlockSpec((ow, seq), lambda b, p: (p, b)),
        out_shape=jax.ShapeDtypeStruct((steps * ow, q.shape[0]), BF16),
        scratch_shapes=[
            pltpu.VMEM((vw // LANES, seq // KEY_CHUNK, PV_ROWS, KEY_CHUNK), BF16),
            pltpu.VMEM((2, seq, Q_TILE), F32),
            pltpu.VMEM((2, seq, Q_TILE), F32),
            pltpu.VMEM((2, PV_ROWS, Q_TILE), F32),
        ],
        compiler_params=pltpu.CompilerParams(vmem_limit_bytes=VMEM_LIMIT_BYTES),
        name=name,
    )(q, k, v)


def _rope_tables(seq):
    rows = seq // GRID_W
    row = jnp.repeat(jnp.arange(rows, dtype=F32), GRID_W)
    col = jnp.tile(jnp.arange(GRID_W, dtype=F32), rows)

    def slot_tables(dim, lead, slot_w):
        half = dim // 2
        inv = ROPE_THETA ** (-jnp.arange(0, half, 2, dtype=F32) / half)
        ar, ac = row[:, None] * inv[None, :], col[:, None] * inv[None, :]
        z = jnp.zeros_like(ar)
        c = jnp.concatenate([jnp.cos(ar)] * 2 + [jnp.cos(ac)] * 2, axis=1)
        sa = jnp.concatenate([-jnp.sin(ar), z, -jnp.sin(ac), z], axis=1)
        sb = jnp.concatenate([z, jnp.sin(ar), z, jnp.sin(ac)], axis=1)
        tail = slot_w - lead - dim
        pad = lambda a, fill: jnp.concatenate(
            [jnp.full((seq, lead), fill, F32), a, jnp.full((seq, tail), fill, F32)], axis=1)
        return pad(c, 1.0), pad(sa, 0.0), pad(sb, 0.0)

    mla = slot_tables(MLA_ROPE, MLA_NOPE, LANES)
    gqa = [jnp.tile(a, (1, LANES // GQA_HD)) for a in slot_tables(GQA_HD, 0, GQA_HD)]
    return jnp.concatenate(list(mla) + gqa, axis=1)


def _layer_params(l, norm_ffn1, w_ffn1_gate, w_ffn1_up, w_ffn1_down, norm_mix, w_in, q_a_norm, w_q_b,
                  kv_a_norm, w_kv_b, gqa_q_norm, gqa_k_norm, w_out, norm_ffn2, w_ffn2_gate, w_ffn2_up,
                  w_ffn2_down):
    row = lambda v: v.reshape(1, -1).astype(F32)
    wi = w_in[l]
    o = np.cumsum([0, Q_LORA, KV_LORA, MLA_ROPE, GQA_HEADS * GQA_HD, GQA_KV_HEADS * GQA_HD, GQA_KV_HEADS * GQA_HD])
    zc = lambda n: jnp.zeros((D_MODEL, n), F32)
    win = jnp.concatenate(
        [wi[:, o[0]:o[1]], wi[:, o[1]:o[2]], zc(MLA_NOPE), wi[:, o[2]:o[3]], zc(LANES - MLA_NOPE - MLA_ROPE),
         wi[:, o[3]:o[4]], wi[:, o[4]:o[5]], wi[:, o[5]:o[6]]], axis=1).astype(BF16)
    assert win.shape[1] == _ZW

    wqb = w_q_b[l].reshape(Q_LORA, MLA_HEADS, MLA_NOPE + MLA_ROPE)
    wqb = jnp.pad(wqb, ((0, 0), (0, 0), (0, LANES - MLA_NOPE - MLA_ROPE))).reshape(Q_LORA, MLA_HEADS * LANES)

    wkv = w_kv_b[l].reshape(KV_LORA, MLA_HEADS, MLA_NOPE + MLA_V)
    wk = jnp.pad(wkv[:, :, :MLA_NOPE], ((0, 0), (0, 0), (0, LANES - MLA_NOPE)))
    wv = jnp.pad(wkv[:, :, MLA_NOPE:], ((0, 0), (0, 0), (0, LANES - MLA_V)))
    wkvb = jnp.concatenate([wk, wv], axis=1).reshape(KV_LORA, 2 * MLA_HEADS * LANES)

    gmat = np.kron(np.eye(GQA_HEADS), np.full((GQA_HD, GQA_HD), 1.0 / GQA_HD))
    return dict(
        g1=row(norm_ffn1[l]), wg1=w_ffn1_gate[l].astype(BF16), wu1=w_ffn1_up[l].astype(BF16),
        wd1=w_ffn1_down[l].astype(BF16),
        gmix=row(norm_mix[l]), win=win, gqa=row(q_a_norm[l]), wqb=wqb.astype(BF16), gkv=row(kv_a_norm[l]),
        wkvb=wkvb.astype(BF16),
        gq=row(jnp.tile(gqa_q_norm[l], GQA_HEADS)) * (GQA_HD ** -0.5 * LOG2E),
        gk=row(jnp.tile(gqa_k_norm[l], GQA_KV_HEADS)),
        gmat=jnp.asarray(gmat, BF16), wout=w_out[l].astype(BF16),
        g2=row(norm_ffn2[l]), wg2=w_ffn2_gate[l].astype(BF16), wu2=w_ffn2_up[l].astype(BF16),
        wd2=w_ffn2_down[l].astype(BF16),
    )


def _trunk(x3, layers, tab, final_g):
    batch, seq, _ = x3.shape
    assert seq % ROW_TILE == 0 and seq % Q_TILE == 0 and seq % GRID_W == 0
    x = x3.reshape(batch * seq, D_MODEL)
    for li, p in enumerate(layers):
        x = _ffn_call(x, p["g1"], p["wg1"], p["wu1"], p["wd1"])
        qm, km, vm, qg, kg, vg = _prep_call(x, tab, seq, p)
        o_mla = _attn_call(qm, km, vm, batch, seq, (0, 1, 2, 3), (0, 1, 2, 3), (0, 1, 2, 3), "attn_mla",
                           single_buffer_inputs=True)
        o_gqa = _attn_call(qg, kg, vg, batch, seq, (0, 0, 1, 1), (0, 1, 0, 1), (0, 0, 0, 0), "attn_gqa")
        x = _ffn_call(x, p["g2"], p["wg2"], p["wu2"], p["wd2"], attn=(o_mla, o_gqa, p["wout"]),
                      final_g=final_g if li == len(layers) - 1 else None)
    return x.reshape(batch, seq, D_MODEL)


def kernel(x_prompt, x_sample, norm_ffn1, w_ffn1_gate, w_ffn1_up, w_ffn1_down, norm_mix, w_in, q_a_norm, w_q_b, kv_a_norm, w_kv_b, gqa_q_norm, gqa_k_norm, w_out, norm_ffn2, w_ffn2_gate, w_ffn2_up, w_ffn2_down, final_norm):
    depth = norm_ffn1.shape[0]
    layers = [
        _layer_params(l, norm_ffn1, w_ffn1_gate, w_ffn1_up, w_ffn1_down, norm_mix, w_in, q_a_norm, w_q_b,
                      kv_a_norm, w_kv_b, gqa_q_norm, gqa_k_norm, w_out, norm_ffn2, w_ffn2_gate, w_ffn2_up,
                      w_ffn2_down)
        for l in range(depth)
    ]
    final_g = final_norm.reshape(1, -1).astype(F32)
    outs = []
    for x3 in (x_prompt, x_sample):
        tab = _rope_tables(x3.shape[1])
        outs.append(_trunk(x3, layers, tab, final_g))
    return tuple(outs)
```
